```python
import jax, jax.numpy as jnp
from jax import lax
import numpy as np

D_MODEL = 2048
BATCH = 16
SEQ = 2048
DEPTH = 4

CHUNK = 64
MIX_WIDTH = D_MODEL
CONV_CH = MIX_WIDTH // 2
CONV_K = 31
RET_HEADS = 4
RET_DK = (MIX_WIDTH // 2) // RET_HEADS
RET_DV = (MIX_WIDTH // 2) // RET_HEADS
RET_QK_W = RET_HEADS * RET_DK
RET_V_W = RET_HEADS * RET_DV
EVEN_IN_W = 2 * CONV_CH + 2 * RET_QK_W + 2 * RET_V_W
EVEN_OUT_IN = CONV_CH + RET_V_W
GM_WIDTH = D_MODEL
GM_BLOCK = 128
GM_GROUPS = 4
D_FF = ((8 * D_MODEL // 3 + 255) // 256) * 256
FFN_K = 3
ROPE_THETA = 10000.0
RMS_EPS = 1e-6
LN_EPS = 1e-5
N_EVEN = (DEPTH + 1) // 2
N_ODD = DEPTH // 2
RESID_SCALE = (2 * DEPTH) ** -0.5

kernel_name = 'hybrid_conv_retention_gmlp_encoder'


def rms_norm(x, g):
    xf = x.astype(jnp.float32)
    y = xf * lax.rsqrt(jnp.mean(xf * xf, axis=-1, keepdims=True) + RMS_EPS)
    return (y * g.astype(jnp.float32)).astype(x.dtype)


def layer_norm(x, g, b):
    xf = x.astype(jnp.float32)
    mu = jnp.mean(xf, axis=-1, keepdims=True)
    var = jnp.mean(jnp.square(xf - mu), axis=-1, keepdims=True)
    y = (xf - mu) * lax.rsqrt(var + LN_EPS)
    return (y * g.astype(jnp.float32) + b.astype(jnp.float32)).astype(x.dtype)


def causal_dwconv(x, w, b):
    k = w.shape[0]
    y = lax.conv_general_dilated(
        x, w[:, None, :].astype(x.dtype), window_strides=(1,), padding=[(k - 1, 0)],
        dimension_numbers=('NWC', 'WIO', 'NWC'), feature_group_count=x.shape[-1])
    return y + b.astype(x.dtype)


def apply_rotary(x):
    s, dh = x.shape[1], x.shape[-1]
    inv = ROPE_THETA ** (-jnp.arange(0, dh, 2, dtype=jnp.float32) / dh)
    ang = jnp.arange(s, dtype=jnp.float32)[:, None] * inv[None, :]
    cos = jnp.cos(ang)[None, :, None, :].astype(x.dtype)
    sin = jnp.sin(ang)[None, :, None, :].astype(x.dtype)
    x1, x2 = x[..., : dh // 2], x[..., dh // 2:]
    return jnp.concatenate([x1 * cos - x2 * sin, x1 * sin + x2 * cos], axis=-1)


def conformer_conv(a, b, w_dw, b_dw, ln_g, ln_b):
    u = a * jax.nn.sigmoid(b)
    u = causal_dwconv(u, w_dw, b_dw)
    return jax.nn.silu(layer_norm(u, ln_g, ln_b))


def retention(q, k, v, gate, ln_g):
    bsz, s, _ = q.shape
    L = CHUNK
    nc = s // L
    H = RET_HEADS
    dt = q.dtype
    q = apply_rotary(q.reshape(bsz, s, H, RET_DK))
    k = apply_rotary(k.reshape(bsz, s, H, RET_DK)) * (RET_DK ** -0.5)
    v = v.reshape(bsz, s, H, RET_DV)
    q = q.reshape(bsz, nc, L, H, RET_DK).transpose(0, 3, 1, 2, 4)
    k = k.reshape(bsz, nc, L, H, RET_DK).transpose(0, 3, 1, 2, 4)
    v = v.reshape(bsz, nc, L, H, RET_DV).transpose(0, 3, 1, 2, 4)

    log_gamma = jnp.log1p(-jnp.exp2(-5.0 - jnp.arange(H, dtype=jnp.float32)))
    pos = jnp.arange(L, dtype=jnp.float32)
    dist = jnp.abs(pos[:, None] - pos[None, :])
    d_intra = jnp.exp(dist[None] * log_gamma[:, None, None]).astype(dt)
    xi = jnp.exp((pos + 1.0)[None] * log_gamma[:, None]).astype(dt)
    zeta = jnp.exp((L - 1.0 - pos)[None] * log_gamma[:, None]).astype(dt)
    g_chunk = jnp.exp(L * log_gamma)

    scores = jnp.einsum('bhcnd,bhcmd->bhcnm', q, k) * d_intra[:, None]
    o_intra = jnp.einsum('bhcnm,bhcme->bhcne', scores, v)

    q_s = jnp.moveaxis(q * xi[:, None, :, None], 2, 0)
    k_s = jnp.moveaxis(k * zeta[:, None, :, None], 2, 0)
    v_s = jnp.moveaxis(v, 2, 0)

    def step(state, inp):
        qc, kc, vc = inp
        cross = jnp.einsum('bhnd,bhde->bhne', qc.astype(jnp.float32), state).astype(dt)
        state = g_chunk[None, :, None, None] * state + jnp.einsum(
            'bhmd,bhme->bhde', kc, vc).astype(jnp.float32)
        return state, cross

    state0 = jnp.zeros((bsz, H, RET_DK, RET_DV), jnp.float32)
    _, o_cross = lax.scan(step, state0, (q_s, k_s, v_s))
    o = o_intra + jnp.moveaxis(o_cross, 0, 2)
    o = o.transpose(0, 2, 3, 1, 4).reshape(bsz, s, H, RET_DV)

    of = o.astype(jnp.float32)
    mu = jnp.mean(of, axis=-1, keepdims=True)
    var = jnp.mean(jnp.square(of - mu), axis=-1, keepdims=True)
    on = ((of - mu) * lax.rsqrt(var + LN_EPS)).reshape(bsz, s, RET_V_W)
    on = (on * ln_g.astype(jnp.float32)).astype(dt)
    return on * jax.nn.silu(gate)


def even_mixer(h, w_in, w_dw, b_dw, cln_g, cln_b, rln_g, w_out):
    p = h @ w_in
    offs = np.cumsum([CONV_CH, CONV_CH, RET_QK_W, RET_QK_W, RET_V_W]).tolist()
    ga, gb, q, k, v, gate = jnp.split(p, offs, axis=-1)
    a_out = conformer_conv(ga, gb, w_dw, b_dw, cln_g, cln_b)
    b_out = retention(q, k, v, gate, rln_g)
    return jnp.concatenate([a_out, b_out], axis=-1) @ w_out


def spatial_gate(v, ws, bs):
    bsz, s, w = v.shape
    nb = s // GM_BLOCK
    vb = v.reshape(bsz, nb, GM_BLOCK, GM_GROUPS, w // GM_GROUPS)
    cidx = jnp.arange(GM_BLOCK) // CHUNK
    mask = cidx[None, :] <= cidx[:, None]
    wm = jnp.where(mask[None], ws, jnp.zeros_like(ws)).astype(v.dtype)
    out = jnp.einsum('gij,bnjgc->bnigc', wm, vb) + bs.T.astype(v.dtype)[None, None, :, :, None]
    return out.reshape(bsz, s, w)


def odd_mixer(h, w_in, ln_g, ln_b, ws, bs, w_out):
    p = jax.nn.gelu(h @ w_in, approximate=False)
    u, v = jnp.split(p, 2, axis=-1)
    v = spatial_gate(layer_norm(v, ln_g, ln_b), ws, bs)
    return (u * v) @ w_out


def conv_ffn(h, w_up, dw_w, dw_b, w_down):
    up = causal_dwconv(h @ w_up, dw_w, dw_b)
    a, b = jnp.split(up, 2, axis=-1)
    return (jax.nn.silu(a) * b) @ w_down


def setup_inputs(seed: int = 0) -> dict:
    key = jax.random.key(seed)
    ks = jax.random.split(key, 24)
    f32 = jnp.float32
    nrm = lambda k, shape, scale: jax.random.normal(k, shape, f32) * scale
    gain = lambda k, shape: 1.0 + 0.02 * jax.random.normal(k, shape, f32)
    return {
        'x': jax.random.normal(ks[0], (BATCH, SEQ, D_MODEL), f32),
        'mix_norm_g': gain(ks[1], (DEPTH, D_MODEL)),
        'ffn_norm_g': gain(ks[2], (DEPTH, D_MODEL)),
        'final_norm_g': gain(ks[3], (D_MODEL,)),
        'ev_w_in': nrm(ks[4], (N_EVEN, D_MODEL, EVEN_IN_W), D_MODEL ** -0.5),
        'ev_conv_dw_w': nrm(ks[5], (N_EVEN, CONV_K, CONV_CH), CONV_K ** -0.5),
        'ev_conv_dw_b': nrm(ks[6], (N_EVEN, CONV_CH), 0.01),
        'ev_conv_ln_g': gain(ks[7], (N_EVEN, CONV_CH)),
        'ev_conv_ln_b': nrm(ks[8], (N_EVEN, CONV_CH), 0.01),
        'ev_ret_ln_g': gain(ks[9], (N_EVEN, RET_V_W)),
        'ev_w_out': nrm(ks[10], (N_EVEN, EVEN_OUT_IN, D_MODEL), EVEN_OUT_IN ** -0.5 * RESID_SCALE),
        'od_w_in': nrm(ks[11], (N_ODD, D_MODEL, 2 * GM_WIDTH), D_MODEL ** -0.5),
        'od_gm_ln_g': gain(ks[12], (N_ODD, GM_WIDTH)),
        'od_gm_ln_b': nrm(ks[13], (N_ODD, GM_WIDTH), 0.01),
        'od_gm_ws': nrm(ks[14], (N_ODD, GM_GROUPS, GM_BLOCK, GM_BLOCK), GM_BLOCK ** -0.5),
        'od_gm_bs': gain(ks[15], (N_ODD, GM_GROUPS, GM_BLOCK)),
        'od_w_out': nrm(ks[16], (N_ODD, GM_WIDTH, D_MODEL), GM_WIDTH ** -0.5 * RESID_SCALE),
        'ffn_w_up': nrm(ks[17], (DEPTH, D_MODEL, 2 * D_FF), D_MODEL ** -0.5),
        'ffn_dw_w': nrm(ks[18], (DEPTH, FFN_K, 2 * D_FF), FFN_K ** -0.5),
        'ffn_dw_b': nrm(ks[19], (DEPTH, 2 * D_FF), 0.01),
        'ffn_w_down': nrm(ks[20], (DEPTH, D_FF, D_MODEL), D_FF ** -0.5 * RESID_SCALE),
    }


def reference(x, mix_norm_g, ffn_norm_g, final_norm_g, ev_w_in, ev_conv_dw_w, ev_conv_dw_b,
              ev_conv_ln_g, ev_conv_ln_b, ev_ret_ln_g, ev_w_out, od_w_in, od_gm_ln_g,
              od_gm_ln_b, od_gm_ws, od_gm_bs, od_w_out, ffn_w_up, ffn_dw_w, ffn_dw_b,
              ffn_w_down):
    for i in range(DEPTH):
        j = i // 2
        h = rms_norm(x, mix_norm_g[i])
        if i % 2 == 0:
            y = even_mixer(h, ev_w_in[j], ev_conv_dw_w[j], ev_conv_dw_b[j], ev_conv_ln_g[j],
                           ev_conv_ln_b[j], ev_ret_ln_g[j], ev_w_out[j])
        else:
            y = odd_mixer(h, od_w_in[j], od_gm_ln_g[j], od_gm_ln_b[j], od_gm_ws[j],
                          od_gm_bs[j], od_w_out[j])
        x = x + y
        h = rms_norm(x, ffn_norm_g[i])
        x = x + conv_ffn(h, ffn_w_up[i], ffn_dw_w[i], ffn_dw_b[i], ffn_w_down[i])
    return rms_norm(x, final_norm_g)
```

```python
import functools

import jax
import jax.numpy as jnp
import numpy as np
from jax import lax
from jax.experimental import pallas as pl
from jax.experimental.pallas import tpu as pltpu

D_MODEL = 2048
SEQ = 2048
DEPTH = 4
CHUNK = 64
CONV_CH = 1024
CONV_K = 31
RET_HEADS = 4
RET_DK = 256
RET_DV = 256
GM_WIDTH = 2048
GM_BLOCK = 128
GM_GROUPS = 4
D_FF = 5632
FFN_K = 3
ROPE_THETA = 10000.0
RMS_EPS = 1e-6
LN_EPS = 1e-5

RET_BLOCK = 256
CONV_HALO = 32
LANES = 128

VMEM_LIMIT = 56 * 1024 * 1024

BF16 = jnp.bfloat16
F32 = jnp.float32


def _params(sem):
    return pltpu.CompilerParams(dimension_semantics=sem, vmem_limit_bytes=VMEM_LIMIT)


def _rms_kernel(x_ref, g_ref, o_ref):
    x = x_ref[...]
    ms = jnp.mean(x * x, axis=-1, keepdims=True)
    o_ref[...] = (x * lax.rsqrt(ms + RMS_EPS) * g_ref[...]).astype(o_ref.dtype)


def rms_norm_bf16(x, g, tm=1024):
    m, d = x.shape
    return pl.pallas_call(
        _rms_kernel,
        grid=(m // tm,),
        in_specs=[pl.BlockSpec((tm, d), lambda i: (i, 0)),
                  pl.BlockSpec((1, d), lambda i: (0, 0))],
        out_specs=pl.BlockSpec((tm, d), lambda i: (i, 0)),
        out_shape=jax.ShapeDtypeStruct((m, d), BF16),
        compiler_params=_params(("parallel",)),
        name="rms_norm",
    )(x, g.reshape(1, d))


def _mm_kernel(a_ref, w_ref, o_ref, *, act):
    acc = jnp.dot(a_ref[...], w_ref[...], preferred_element_type=F32)
    if act == "gelu":
        acc = 0.5 * acc * (1.0 + lax.erf(acc * np.float32(np.sqrt(0.5))))
    o_ref[...] = acc.astype(o_ref.dtype)


def matmul(a, w, *, act=None, out_dtype=F32, tm=1024, tn=512, name="proj"):
    m, k = a.shape
    n = w.shape[1]
    return pl.pallas_call(
        functools.partial(_mm_kernel, act=act),
        grid=(m // tm, n // tn),
        in_specs=[pl.BlockSpec((tm, k), lambda i, j: (i, 0)),
                  pl.BlockSpec((k, tn), lambda i, j: (0, j))],
        out_specs=pl.BlockSpec((tm, tn), lambda i, j: (i, j)),
        out_shape=jax.ShapeDtypeStruct((m, n), out_dtype),
        compiler_params=_params(("parallel", "arbitrary")),
        name=name,
    )(a, w)


def _mm_res_kernel(a_ref, w_ref, x_ref, g_ref, xo_ref, ho_ref, *, nk):
    k = pl.program_id(1)

    @pl.when(k == 0)
    def _():
        xo_ref[...] = x_ref[...]

    xo_ref[...] += jnp.dot(a_ref[...], w_ref[...], preferred_element_type=F32)

    @pl.when(k == nk - 1)
    def _():
        xn = xo_ref[...]
        ms = jnp.mean(xn * xn, axis=-1, keepdims=True)
        ho_ref[...] = (xn * lax.rsqrt(ms + RMS_EPS) * g_ref[...]).astype(ho_ref.dtype)


def matmul_residual_norm(a, w, x, g, *, h_dtype=BF16, tm=1024, tk=512, name="proj_res"):
    m, kdim = a.shape
    d = w.shape[1]
    nk = kdim // tk
    return pl.pallas_call(
        functools.partial(_mm_res_kernel, nk=nk),
        grid=(m // tm, nk),
        in_specs=[pl.BlockSpec((tm, tk), lambda i, k: (i, k)),
                  pl.BlockSpec((tk, d), lambda i, k: (k, 0)),
                  pl.BlockSpec((tm, d), lambda i, k: (i, 0)),
                  pl.BlockSpec((1, d), lambda i, k: (0, 0))],
        out_specs=[pl.BlockSpec((tm, d), lambda i, k: (i, 0)),
                   pl.BlockSpec((tm, d), lambda i, k: (i, 0))],
        out_shape=[jax.ShapeDtypeStruct((m, d), F32),
                   jax.ShapeDtypeStruct((m, d), h_dtype)],
        compiler_params=_params(("parallel", "arbitrary")),
        name=name,
    )(a, w, x, g.reshape(1, d))


def _causal_conv3(u, cw, cb):
    rows = lax.broadcasted_iota(jnp.int32, u.shape, 0)
    u1 = jnp.where(rows >= 1, pltpu.roll(u, 1, 0), 0.0)
    u2 = jnp.where(rows >= 2, pltpu.roll(u, 2, 0), 0.0)
    return cw[2:3, :] * u + cw[1:2, :] * u1 + cw[0:1, :] * u2 + cb


def _ffn_up_kernel(h_ref, wa_ref, wb_ref, cwa_ref, cwb_ref, cba_ref, cbb_ref, o_ref):
    h = h_ref[...]
    ua = jnp.dot(h, wa_ref[...], preferred_element_type=F32)
    ub = jnp.dot(h, wb_ref[...], preferred_element_type=F32)
    a = _causal_conv3(ua, cwa_ref[...], cba_ref[...])
    b = _causal_conv3(ub, cwb_ref[...], cbb_ref[...])
    o_ref[...] = (a * jax.nn.sigmoid(a) * b).astype(o_ref.dtype)


def ffn_up_gate(h, w_up, dw_w, dw_b, *, tn=256):
    m, k = h.shape
    nj = D_FF // tn
    dw_b = dw_b.reshape(1, 2 * D_FF)
    return pl.pallas_call(
        _ffn_up_kernel,
        grid=(m // SEQ, nj),
        in_specs=[pl.BlockSpec((SEQ, k), lambda i, j: (i, 0)),
                  pl.BlockSpec((k, tn), lambda i, j: (0, j)),
                  pl.BlockSpec((k, tn), lambda i, j: (0, j + nj)),
                  pl.BlockSpec((FFN_K, tn), lambda i, j: (0, j)),
                  pl.BlockSpec((FFN_K, tn), lambda i, j: (0, j + nj)),
                  pl.BlockSpec((1, tn), lambda i, j: (0, j)),
                  pl.BlockSpec((1, tn), lambda i, j: (0, j + nj))],
        out_specs=pl.BlockSpec((SEQ, tn), lambda i, j: (i, j)),
        out_shape=jax.ShapeDtypeStruct((m, D_FF), BF16),
        compiler_params=_params(("parallel", "arbitrary")),
        name="ffn_up_gate",
    )(h, w_up, w_up, dw_w, dw_w, dw_b, dw_b)


def _conformer_kernel(ga_ref, gb_ref, ha_ref, hb_ref, w_ref, b_ref, lg_ref, lb_ref, o_ref,
                      u_ref, y_ref, *, ts, tiles_per_seq):
    i = pl.program_id(0)
    nc = CONV_CH // LANES
    not_first = (i % tiles_per_seq != 0).astype(F32)
    uh = ha_ref[...] * jax.nn.sigmoid(hb_ref[...]) * not_first
    u = ga_ref[...] * jax.nn.sigmoid(gb_ref[...])
    for c in range(nc):
        lanes = slice(c * LANES, (c + 1) * LANES)
        u_ref[c, 0:CONV_HALO, :] = uh[:, lanes]
        u_ref[c, CONV_HALO:CONV_HALO + ts, :] = u[:, lanes]
    base = CONV_HALO - (CONV_K - 1)
    rc = 64
    for c in range(nc):
        lanes = slice(c * LANES, (c + 1) * LANES)

        def body(r, carry, c=c, lanes=lanes):
            r0 = pl.multiple_of(r * rc, rc)
            acc = jnp.broadcast_to(b_ref[:, lanes], (rc, LANES))
            for k in range(CONV_K):
                acc = acc + w_ref[k:k + 1, lanes] * u_ref[c, pl.ds(r0 + base + k, rc), :]
            y_ref[pl.ds(r0, rc), lanes] = acc
            return carry

        lax.fori_loop(0, ts // rc, body, 0)
    y = y_ref[...]
    mu = jnp.mean(y, axis=-1, keepdims=True)
    yc = y - mu
    var = jnp.mean(yc * yc, axis=-1, keepdims=True)
    yn = yc * lax.rsqrt(var + LN_EPS) * lg_ref[...] + lb_ref[...]
    o_ref[...] = (yn * jax.nn.sigmoid(yn)).astype(o_ref.dtype)


def conformer_conv(p, w_dw, b_dw, ln_g, ln_b, *, ts=512):
    m = p.shape[0]
    tiles_per_seq = SEQ // ts
    halo_per_tile = ts // CONV_HALO

    def halo_idx(col):
        return lambda i: (jnp.maximum(i * halo_per_tile - 1, 0), col)

    return pl.pallas_call(
        functools.partial(_conformer_kernel, ts=ts, tiles_per_seq=tiles_per_seq),
        grid=(m // ts,),
        in_specs=[pl.BlockSpec((ts, CONV_CH), lambda i: (i, 0)),
                  pl.BlockSpec((ts, CONV_CH), lambda i: (i, 1)),
                  pl.BlockSpec((CONV_HALO, CONV_CH), halo_idx(0)),
                  pl.BlockSpec((CONV_HALO, CONV_CH), halo_idx(1)),
                  pl.BlockSpec((CONV_K, CONV_CH), lambda i: (0, 0)),
                  pl.BlockSpec((1, CONV_CH), lambda i: (0, 0)),
                  pl.BlockSpec((1, CONV_CH), lambda i: (0, 0)),
                  pl.BlockSpec((1, CONV_CH), lambda i: (0, 0))],
        out_specs=pl.BlockSpec((ts, CONV_CH), lambda i: (i, 0)),
        out_shape=jax.ShapeDtypeStruct((m, CONV_CH), BF16),
        scratch_shapes=[pltpu.VMEM((CONV_CH // LANES, CONV_HALO + ts, LANES), F32),
                        pltpu.VMEM((ts, CONV_CH), F32)],
        compiler_params=_params(("parallel",)),
        name="conformer_conv",
    )(p, p, p, p, w_dw, b_dw.reshape(1, CONV_CH), ln_g.reshape(1, CONV_CH),
      ln_b.reshape(1, CONV_CH))


def _retention_tables():
    t = RET_BLOCK
    log_gamma = jnp.log1p(-jnp.exp2(-5.0 - jnp.arange(RET_HEADS, dtype=F32)))
    pos = jnp.arange(t, dtype=F32)
    dist = jnp.abs(pos[:, None] - pos[None, :])
    chunk = jnp.arange(t) // CHUNK
    mask = chunk[None, :] <= chunk[:, None]
    d = jnp.where(mask[None], jnp.exp(dist[None] * log_gamma[:, None, None]), 0.0)
    xi = jnp.exp((pos + 1.0)[None] * log_gamma[:, None])
    zeta = jnp.exp((t - 1.0 - pos)[None] * log_gamma[:, None])
    gt = jnp.exp(t * log_gamma)
    xi = jnp.broadcast_to(xi[:, :, None], (RET_HEADS, t, RET_DV))
    zeta = jnp.broadcast_to(zeta[:, :, None], (RET_HEADS, t, RET_DK))
    gt = jnp.broadcast_to(gt[:, None, None], (RET_HEADS, 1, RET_DV))
    inv = ROPE_THETA ** (-jnp.arange(0, RET_DK, 2, dtype=F32) / RET_DK)
    ang = jnp.arange(SEQ, dtype=F32)[:, None] * inv[None, :]
    return d, xi, zeta, gt, jnp.cos(ang), jnp.sin(ang)


def _rotary(x, cos, sin):
    half = x.shape[-1] // 2
    x1, x2 = x[:, :half], x[:, half:]
    return jnp.concatenate([x1 * cos - x2 * sin, x1 * sin + x2 * cos], axis=-1)


def _retention_kernel(q_ref, k_ref, v_ref, gate_ref, cos_ref, sin_ref, d_ref, xi_ref, zeta_ref,
                      gt_ref, lg_ref, o_ref, state_ref):
    t = RET_BLOCK
    state_ref[...] = jnp.zeros_like(state_ref)
    scale = np.float32(RET_DK ** -0.5)

    def body(blk, carry):
        r0 = pl.multiple_of(blk * t, t)
        rows = pl.ds(r0, t)
        cos = cos_ref[rows, :]
        sin = sin_ref[rows, :]
        q = _rotary(q_ref[rows, :], cos, sin)
        k = _rotary(k_ref[rows, :], cos, sin) * scale
        v = v_ref[rows, :].astype(BF16)
        qb = q.astype(BF16)
        s = lax.dot_general(qb, k.astype(BF16), (((1,), (1,)), ((), ())),
                            preferred_element_type=F32) * d_ref[...]
        state = state_ref[...]
        o = jnp.dot(s.astype(BF16), v, preferred_element_type=F32)
        o = o + xi_ref[...] * jnp.dot(qb, state.astype(BF16), preferred_element_type=F32)
        kz = (k * zeta_ref[...]).astype(BF16)
        state_ref[...] = gt_ref[...] * state + lax.dot_general(
            kz, v, (((0,), (0,)), ((), ())), preferred_element_type=F32)
        mu = jnp.mean(o, axis=-1, keepdims=True)
        oc = o - mu
        var = jnp.mean(oc * oc, axis=-1, keepdims=True)
        on = oc * lax.rsqrt(var + LN_EPS) * lg_ref[...]
        g = gate_ref[rows, :]
        o_ref[rows, :] = (on * (g * jax.nn.sigmoid(g))).astype(o_ref.dtype)
        return carry

    lax.fori_loop(0, SEQ // t, body, 0)


def retention(p, ln_g):
    m = p.shape[0]
    d, xi, zeta, gt, cos, sin = _retention_tables()
    t = RET_BLOCK
    col0 = 2 * CONV_CH // RET_DK

    def col(off):
        return lambda b, h: (b, col0 + off * RET_HEADS + h)

    return pl.pallas_call(
        _retention_kernel,
        grid=(m // SEQ, RET_HEADS),
        in_specs=[pl.BlockSpec((SEQ, RET_DK), col(0)),
                  pl.BlockSpec((SEQ, RET_DK), col(1)),
                  pl.BlockSpec((SEQ, RET_DV), col(2)),
                  pl.BlockSpec((SEQ, RET_DV), col(3)),
                  pl.BlockSpec((SEQ, RET_DK // 2), lambda b, h: (0, 0)),
                  pl.BlockSpec((SEQ, RET_DK // 2), lambda b, h: (0, 0)),
                  pl.BlockSpec((None, t, t), lambda b, h: (h, 0, 0)),
                  pl.BlockSpec((None, t, RET_DV), lambda b, h: (h, 0, 0)),
                  pl.BlockSpec((None, t, RET_DK), lambda b, h: (h, 0, 0)),
                  pl.BlockSpec((None, 1, RET_DV), lambda b, h: (h, 0, 0)),
                  pl.BlockSpec((1, RET_DV), lambda b, h: (0, h))],
        out_specs=pl.BlockSpec((SEQ, RET_DV), lambda b, h: (b, h)),
        out_shape=jax.ShapeDtypeStruct((m, RET_HEADS * RET_DV), BF16),
        scratch_shapes=[pltpu.VMEM((RET_DK, RET_DV), F32)],
        compiler_params=_params(("parallel", "arbitrary")),
        name="retention",
    )(p, p, p, p, cos, sin, d, xi, zeta, gt, ln_g.reshape(1, RET_HEADS * RET_DV))


def _sgu_kernel(u_ref, v_ref, lg_ref, lb_ref, ws_ref, bs_ref, o_ref, vn_ref, *, tm):
    v = v_ref[...]
    mu = jnp.mean(v, axis=-1, keepdims=True)
    vc = v - mu
    var = jnp.mean(vc * vc, axis=-1, keepdims=True)
    vn_ref[...] = (vc * lax.rsqrt(var + LN_EPS) * lg_ref[...] + lb_ref[...]).astype(BF16)
    ri = lax.broadcasted_iota(jnp.int32, (GM_BLOCK, GM_BLOCK), 0) // CHUNK
    ci = lax.broadcasted_iota(jnp.int32, (GM_BLOCK, GM_BLOCK), 1) // CHUNK
    gw = GM_WIDTH // GM_GROUPS
    for g in range(GM_GROUPS):
        wm = jnp.where(ci <= ri, ws_ref[g], 0.0).astype(BF16)
        cols = slice(g * gw, (g + 1) * gw)
        for n in range(tm // GM_BLOCK):
            rows = slice(n * GM_BLOCK, (n + 1) * GM_BLOCK)
            mixed = jnp.dot(wm, vn_ref[rows, cols], preferred_element_type=F32) + bs_ref[g]
            o_ref[rows, cols] = (u_ref[rows, cols] * mixed).astype(o_ref.dtype)


def spatial_gate(p, ln_g, ln_b, ws, bs, *, tm=512):
    m = p.shape[0]
    return pl.pallas_call(
        functools.partial(_sgu_kernel, tm=tm),
        grid=(m // tm,),
        in_specs=[pl.BlockSpec((tm, GM_WIDTH), lambda i: (i, 0)),
                  pl.BlockSpec((tm, GM_WIDTH), lambda i: (i, 1)),
                  pl.BlockSpec((1, GM_WIDTH), lambda i: (0, 0)),
                  pl.BlockSpec((1, GM_WIDTH), lambda i: (0, 0)),
                  pl.BlockSpec((GM_GROUPS, GM_BLOCK, GM_BLOCK), lambda i: (0, 0, 0)),
                  pl.BlockSpec((GM_GROUPS, GM_BLOCK, 1), lambda i: (0, 0, 0))],
        out_specs=pl.BlockSpec((tm, GM_WIDTH), lambda i: (i, 0)),
        out_shape=jax.ShapeDtypeStruct((m, GM_WIDTH), BF16),
        scratch_shapes=[pltpu.VMEM((tm, GM_WIDTH), BF16)],
        compiler_params=_params(("parallel",)),
        name="spatial_gate",
    )(p, p, ln_g.reshape(1, GM_WIDTH), ln_b.reshape(1, GM_WIDTH), ws,
      bs.reshape(GM_GROUPS, GM_BLOCK, 1))


def kernel(x, mix_norm_g, ffn_norm_g, final_norm_g, ev_w_in, ev_conv_dw_w, ev_conv_dw_b,
           ev_conv_ln_g, ev_conv_ln_b, ev_ret_ln_g, ev_w_out, od_w_in, od_gm_ln_g, od_gm_ln_b,
           od_gm_ws, od_gm_bs, od_w_out, ffn_w_up, ffn_dw_w, ffn_dw_b, ffn_w_down):
    bsz, s, d = x.shape
    xf = x.reshape(bsz * s, d)
    h = rms_norm_bf16(xf, mix_norm_g[0])
    for i in range(DEPTH):
        j = i // 2
        if i % 2 == 0:
            p = matmul(h, ev_w_in[j].astype(BF16), name="even_in")
            a_out = conformer_conv(p, ev_conv_dw_w[j], ev_conv_dw_b[j], ev_conv_ln_g[j],
                                   ev_conv_ln_b[j])
            b_out = retention(p, ev_ret_ln_g[j])
            z = jnp.concatenate([a_out, b_out], axis=-1)
            w_out = ev_w_out[j]
        else:
            p = matmul(h, od_w_in[j].astype(BF16), act="gelu", name="odd_in")
            z = spatial_gate(p, od_gm_ln_g[j], od_gm_ln_b[j], od_gm_ws[j], od_gm_bs[j])
            w_out = od_w_out[j]
        xf, h = matmul_residual_norm(z, w_out.astype(BF16), xf, ffn_norm_g[i], name="mix_out")
        gated = ffn_up_gate(h, ffn_w_up[i].astype(BF16), ffn_dw_w[i], ffn_dw_b[i])
        last = i == DEPTH - 1
        g_next = final_norm_g if last else mix_norm_g[i + 1]
        xf, h = matmul_residual_norm(gated, ffn_w_down[i].astype(BF16), xf, g_next,
                                     h_dtype=F32 if last else BF16, name="ffn_down")
    return h.reshape(bsz, s, d)
```

```python
import functools

import jax
import jax.numpy as jnp
import numpy as np
from jax import lax
from jax.experimental import pallas as pl
from jax.experimental.pallas import tpu as pltpu

D_MODEL = 2048
SEQ = 2048
DEPTH = 4
CHUNK = 64
CONV_CH = 1024
CONV_K = 31
RET_HEADS = 4
RET_DK = 256
RET_DV = 256
GM_WIDTH = 2048
GM_BLOCK = 128
GM_GROUPS = 4
D_FF = 5632
FFN_K = 3
ROPE_THETA = 10000.0
RMS_EPS = 1e-6
LN_EPS = 1e-5

RET_BLOCK = 256
CONV_HALO = 32
LANES = 128

VMEM_LIMIT = 56 * 1024 * 1024

BF16 = jnp.bfloat16
F32 = jnp.float32


def _params(sem):
    return pltpu.CompilerParams(dimension_semantics=sem, vmem_limit_bytes=VMEM_LIMIT)


def _rms_kernel(x_ref, g_ref, o_ref):
    x = x_ref[...]
    ms = jnp.mean(x * x, axis=-1, keepdims=True)
    o_ref[...] = (x * lax.rsqrt(ms + RMS_EPS) * g_ref[...]).astype(o_ref.dtype)


def rms_norm_bf16(x, g, tm=1024):
    m, d = x.shape
    return pl.pallas_call(
        _rms_kernel,
        grid=(m // tm,),
        in_specs=[pl.BlockSpec((tm, d), lambda i: (i, 0)),
                  pl.BlockSpec((1, d), lambda i: (0, 0))],
        out_specs=pl.BlockSpec((tm, d), lambda i: (i, 0)),
        out_shape=jax.ShapeDtypeStruct((m, d), BF16),
        compiler_params=_params(("parallel",)),
        name="rms_norm",
    )(x, g.reshape(1, d))


def _mm_kernel(a_ref, w_ref, o_ref, *, act):
    acc = jnp.dot(a_ref[...], w_ref[...], preferred_element_type=F32)
    if act == "gelu":
        acc = 0.5 * acc * (1.0 + lax.erf(acc * np.float32(np.sqrt(0.5))))
    o_ref[...] = acc.astype(o_ref.dtype)


def matmul(a, w, *, act=None, out_dtype=F32, tm=1024, tn=1024, name="proj"):
    m, k = a.shape
    n = w.shape[1]
    return pl.pallas_call(
        functools.partial(_mm_kernel, act=act),
        grid=(m // tm, n // tn),
        in_specs=[pl.BlockSpec((tm, k), lambda i, j: (i, 0)),
                  pl.BlockSpec((k, tn), lambda i, j: (0, j))],
        out_specs=pl.BlockSpec((tm, tn), lambda i, j: (i, j)),
        out_shape=jax.ShapeDtypeStruct((m, n), out_dtype),
        compiler_params=_params(("parallel", "arbitrary")),
        name=name,
    )(a, w)


def _mm_res_kernel(*refs, n_a, nk, tm, rc, emit_x):
    a_refs, w_refs = refs[:n_a], refs[n_a:2 * n_a]
    x_ref, g_ref = refs[2 * n_a:2 * n_a + 2]
    if emit_x:
        xo_ref, ho_ref = refs[2 * n_a + 2:]
        acc_ref = xo_ref
    else:
        ho_ref, acc_ref = refs[2 * n_a + 2:]
    k = pl.program_id(1)

    def partial_product(rows):
        out = None
        for a_ref, w_ref in zip(a_refs, w_refs):
            p = jnp.dot(a_ref[rows, :], w_ref[...], preferred_element_type=F32)
            out = p if out is None else out + p
        return out

    def finish():
        for r in range(tm // rc):
            rows = slice(r * rc, (r + 1) * rc)
            xn = x_ref[rows, :] + partial_product(rows)
            if nk > 1:
                xn = xn + acc_ref[rows, :]
            if emit_x:
                xo_ref[rows, :] = xn
            ms = jnp.mean(xn * xn, axis=-1, keepdims=True)
            ho_ref[rows, :] = (xn * lax.rsqrt(ms + RMS_EPS) * g_ref[...]).astype(ho_ref.dtype)

    if nk == 1:
        finish()
        return

    @pl.when(k == 0)
    def _():
        acc_ref[...] = partial_product(slice(None))

    @pl.when(jnp.logical_and(k > 0, k < nk - 1))
    def _():
        acc_ref[...] += partial_product(slice(None))

    @pl.when(k == nk - 1)
    def _():
        finish()


def matmul_residual_norm(a_list, w, x, g, *, h_dtype=BF16, emit_x=True, tm=1024, tk=512, rc=256,
                         name="proj_res"):
    n_a = len(a_list)
    m, ka = a_list[0].shape
    d = w.shape[1]
    nk = ka // tk
    a_specs = [pl.BlockSpec((tm, tk), lambda i, k: (i, k)) for _ in range(n_a)]
    w_specs = [pl.BlockSpec((tk, d), lambda i, k, q=q: (q * nk + k, 0)) for q in range(n_a)]
    row_spec = pl.BlockSpec((tm, d), lambda i, k: (i, 0))
    if emit_x:
        out_specs = [row_spec, row_spec]
        out_shape = [jax.ShapeDtypeStruct((m, d), F32), jax.ShapeDtypeStruct((m, d), h_dtype)]
        scratch = []
    else:
        out_specs = row_spec
        out_shape = jax.ShapeDtypeStruct((m, d), h_dtype)
        scratch = [pltpu.VMEM((tm, d), F32)]
    return pl.pallas_call(
        functools.partial(_mm_res_kernel, n_a=n_a, nk=nk, tm=tm, rc=rc, emit_x=emit_x),
        grid=(m // tm, nk),
        in_specs=a_specs + w_specs + [row_spec, pl.BlockSpec((1, d), lambda i, k: (0, 0))],
        out_specs=out_specs,
        out_shape=out_shape,
        scratch_shapes=scratch,
        compiler_params=_params(("parallel", "arbitrary")),
        name=name,
    )(*a_list, *([w] * n_a), x, g.reshape(1, d))


FFN_PAD = 8


def _ffn_up_kernel(h_ref, wa_ref, wb_ref, cwa_ref, cwb_ref, cba_ref, cbb_ref, o_ref, s_ref,
                   *, rc, tn):
    nt = tn // LANES
    w_refs = (wa_ref, wb_ref)
    cw = (cwa_ref[...], cwb_ref[...])
    cb = (cba_ref[...], cbb_ref[...])
    for ab in range(2):
        for c in range(nt):
            s_ref[ab, c, 0:FFN_PAD, :] = jnp.zeros((FFN_PAD, LANES), F32)
    for r in range(SEQ // rc):
        r0 = r * rc
        hr = h_ref[r0:r0 + rc, :]
        u = [jnp.dot(hr, w_refs[ab][...], preferred_element_type=F32) for ab in range(2)]
        for ab in range(2):
            for c in range(nt):
                s_ref[ab, c, FFN_PAD + r0:FFN_PAD + r0 + rc, :] = u[ab][:, c * LANES:(c + 1) * LANES]
        for c in range(nt):
            lanes = slice(c * LANES, (c + 1) * LANES)
            y = []
            for ab in range(2):
                u1 = s_ref[ab, c, FFN_PAD + r0 - 1:FFN_PAD + r0 - 1 + rc, :]
                u2 = s_ref[ab, c, FFN_PAD + r0 - 2:FFN_PAD + r0 - 2 + rc, :]
                y.append(cw[ab][2:3, lanes] * u[ab][:, lanes] + cw[ab][1:2, lanes] * u1
                         + cw[ab][0:1, lanes] * u2 + cb[ab][:, lanes])
            a, b = y
            o_ref[r0:r0 + rc, lanes] = (a * jax.nn.sigmoid(a) * b).astype(o_ref.dtype)


def ffn_up_gate(h, w_up, dw_w, dw_b, *, tn=512, rc=512):
    m, k = h.shape
    nj = D_FF // tn
    dw_b = dw_b.reshape(1, 2 * D_FF)
    return pl.pallas_call(
        functools.partial(_ffn_up_kernel, rc=rc, tn=tn),
        grid=(m // SEQ, nj),
        in_specs=[pl.BlockSpec((SEQ, k), lambda i, j: (i, 0)),
                  pl.BlockSpec((k, tn), lambda i, j: (0, j)),
                  pl.BlockSpec((k, tn), lambda i, j: (0, j + nj)),
                  pl.BlockSpec((FFN_K, tn), lambda i, j: (0, j)),
                  pl.BlockSpec((FFN_K, tn), lambda i, j: (0, j + nj)),
                  pl.BlockSpec((1, tn), lambda i, j: (0, j)),
                  pl.BlockSpec((1, tn), lambda i, j: (0, j + nj))],
        out_specs=pl.BlockSpec((SEQ, tn), lambda i, j: (i, j)),
        out_shape=jax.ShapeDtypeStruct((m, D_FF), BF16),
        scratch_shapes=[pltpu.VMEM((2, tn // LANES, FFN_PAD + SEQ, LANES), F32)],
        compiler_params=_params(("parallel", "arbitrary")),
        name="ffn_up_gate",
    )(h, w_up, w_up, dw_w, dw_w, dw_b, dw_b)


def _conformer_kernel(ga_ref, gb_ref, ha_ref, hb_ref, w_ref, b_ref, lg_ref, lb_ref, o_ref,
                      u_ref, y_ref, *, ts, tiles_per_seq):
    i = pl.program_id(0)
    nc = CONV_CH // LANES
    not_first = (i % tiles_per_seq != 0).astype(F32)
    uh = ha_ref[...] * jax.nn.sigmoid(hb_ref[...]) * not_first
    u = ga_ref[...] * jax.nn.sigmoid(gb_ref[...])
    for c in range(nc):
        lanes = slice(c * LANES, (c + 1) * LANES)
        u_ref[c, 0:CONV_HALO, :] = uh[:, lanes]
        u_ref[c, CONV_HALO:CONV_HALO + ts, :] = u[:, lanes]
    base = CONV_HALO - (CONV_K - 1)
    rc = 64
    for c in range(nc):
        lanes = slice(c * LANES, (c + 1) * LANES)

        def body(r, carry, c=c, lanes=lanes):
            r0 = pl.multiple_of(r * rc, rc)
            acc = jnp.broadcast_to(b_ref[:, lanes], (rc, LANES))
            for k in range(CONV_K):
                acc = acc + w_ref[k:k + 1, lanes] * u_ref[c, pl.ds(r0 + base + k, rc), :]
            y_ref[pl.ds(r0, rc), lanes] = acc
            return carry

        lax.fori_loop(0, ts // rc, body, 0)
    y = y_ref[...]
    mu = jnp.mean(y, axis=-1, keepdims=True)
    yc = y - mu
    var = jnp.mean(yc * yc, axis=-1, keepdims=True)
    yn = yc * lax.rsqrt(var + LN_EPS) * lg_ref[...] + lb_ref[...]
    o_ref[...] = (yn * jax.nn.sigmoid(yn)).astype(o_ref.dtype)


def conformer_conv(p, w_dw, b_dw, ln_g, ln_b, *, ts=512):
    m = p.shape[0]
    tiles_per_seq = SEQ // ts
    halo_per_tile = ts // CONV_HALO

    def halo_idx(col):
        return lambda i: (jnp.maximum(i * halo_per_tile - 1, 0), col)

    return pl.pallas_call(
        functools.partial(_conformer_kernel, ts=ts, tiles_per_seq=tiles_per_seq),
        grid=(m // ts,),
        in_specs=[pl.BlockSpec((ts, CONV_CH), lambda i: (i, 0)),
                  pl.BlockSpec((ts, CONV_CH), lambda i: (i, 1)),
                  pl.BlockSpec((CONV_HALO, CONV_CH), halo_idx(0)),
                  pl.BlockSpec((CONV_HALO, CONV_CH), halo_idx(1)),
                  pl.BlockSpec((CONV_K, CONV_CH), lambda i: (0, 0)),
                  pl.BlockSpec((1, CONV_CH), lambda i: (0, 0)),
                  pl.BlockSpec((1, CONV_CH), lambda i: (0, 0)),
                  pl.BlockSpec((1, CONV_CH), lambda i: (0, 0))],
        out_specs=pl.BlockSpec((ts, CONV_CH), lambda i: (i, 0)),
        out_shape=jax.ShapeDtypeStruct((m, CONV_CH), BF16),
        scratch_shapes=[pltpu.VMEM((CONV_CH // LANES, CONV_HALO + ts, LANES), F32),
                        pltpu.VMEM((ts, CONV_CH), F32)],
        compiler_params=_params(("parallel",)),
        name="conformer_conv",
    )(p, p, p, p, w_dw, b_dw.reshape(1, CONV_CH), ln_g.reshape(1, CONV_CH),
      ln_b.reshape(1, CONV_CH))


def _retention_tables():
    t = RET_BLOCK
    log_gamma = jnp.log1p(-jnp.exp2(-5.0 - jnp.arange(RET_HEADS, dtype=F32)))
    pos = jnp.arange(t, dtype=F32)
    dist = jnp.abs(pos[:, None] - pos[None, :])
    chunk = jnp.arange(t) // CHUNK
    mask = chunk[None, :] <= chunk[:, None]
    d = jnp.where(mask[None], jnp.exp(dist[None] * log_gamma[:, None, None]), 0.0)
    xi = jnp.exp((pos + 1.0)[None] * log_gamma[:, None])
    zeta = jnp.exp((t - 1.0 - pos)[None] * log_gamma[:, None])
    gt = jnp.exp(t * log_gamma)
    xi = jnp.broadcast_to(xi[:, :, None], (RET_HEADS, t, RET_DV))
    zeta = jnp.broadcast_to(zeta[:, :, None], (RET_HEADS, t, RET_DK))
    gt = jnp.broadcast_to(gt[:, None, None], (RET_HEADS, 1, RET_DV))
    inv = ROPE_THETA ** (-jnp.arange(0, RET_DK, 2, dtype=F32) / RET_DK)
    ang = jnp.arange(SEQ, dtype=F32)[:, None] * inv[None, :]
    return d, xi, zeta, gt, jnp.cos(ang), jnp.sin(ang)


def _rotary(x, cos, sin):
    half = x.shape[-1] // 2
    x1, x2 = x[:, :half], x[:, half:]
    return jnp.concatenate([x1 * cos - x2 * sin, x1 * sin + x2 * cos], axis=-1)


def _retention_kernel(q_ref, k_ref, v_ref, gate_ref, cos_ref, sin_ref, d_ref, xi_ref, zeta_ref,
                      gt_ref, lg_ref, o_ref, state_ref):
    t = RET_BLOCK
    state_ref[...] = jnp.zeros_like(state_ref)
    scale = np.float32(RET_DK ** -0.5)

    def body(blk, carry):
        r0 = pl.multiple_of(blk * t, t)
        rows = pl.ds(r0, t)
        cos = cos_ref[rows, :]
        sin = sin_ref[rows, :]
        q = _rotary(q_ref[rows, :], cos, sin)
        k = _rotary(k_ref[rows, :], cos, sin) * scale
        v = v_ref[rows, :].astype(BF16)
        qb = q.astype(BF16)
        s = lax.dot_general(qb, k.astype(BF16), (((1,), (1,)), ((), ())),
                            preferred_element_type=F32) * d_ref[...]
        state = state_ref[...]
        o = jnp.dot(s.astype(BF16), v, preferred_element_type=F32)
        o = o + xi_ref[...] * jnp.dot(qb, state.astype(BF16), preferred_element_type=F32)
        kz = (k * zeta_ref[...]).astype(BF16)
        state_ref[...] = gt_ref[...] * state + lax.dot_general(
            kz, v, (((0,), (0,)), ((), ())), preferred_element_type=F32)
        mu = jnp.mean(o, axis=-1, keepdims=True)
        oc = o - mu
        var = jnp.mean(oc * oc, axis=-1, keepdims=True)
        on = oc * lax.rsqrt(var + LN_EPS) * lg_ref[...]
        g = gate_ref[rows, :]
        o_ref[rows, :] = (on * (g * jax.nn.sigmoid(g))).astype(o_ref.dtype)
        return carry

    lax.fori_loop(0, SEQ // t, body, 0)


def retention(p, ln_g):
    m = p.shape[0]
    d, xi, zeta, gt, cos, sin = _retention_tables()
    t = RET_BLOCK
    col0 = 2 * CONV_CH // RET_DK

    def col(off):
        return lambda b, h: (b, col0 + off * RET_HEADS + h)

    return pl.pallas_call(
        _retention_kernel,
        grid=(m // SEQ, RET_HEADS),
        in_specs=[pl.BlockSpec((SEQ, RET_DK), col(0)),
                  pl.BlockSpec((SEQ, RET_DK), col(1)),
                  pl.BlockSpec((SEQ, RET_DV), col(2)),
                  pl.BlockSpec((SEQ, RET_DV), col(3)),
                  pl.BlockSpec((SEQ, RET_DK // 2), lambda b, h: (0, 0)),
                  pl.BlockSpec((SEQ, RET_DK // 2), lambda b, h: (0, 0)),
                  pl.BlockSpec((None, t, t), lambda b, h: (h, 0, 0)),
                  pl.BlockSpec((None, t, RET_DV), lambda b, h: (h, 0, 0)),
                  pl.BlockSpec((None, t, RET_DK), lambda b, h: (h, 0, 0)),
                  pl.BlockSpec((None, 1, RET_DV), lambda b, h: (h, 0, 0)),
                  pl.BlockSpec((1, RET_DV), lambda b, h: (0, h))],
        out_specs=pl.BlockSpec((SEQ, RET_DV), lambda b, h: (b, h)),
        out_shape=jax.ShapeDtypeStruct((m, RET_HEADS * RET_DV), BF16),
        scratch_shapes=[pltpu.VMEM((RET_DK, RET_DV), F32)],
        compiler_params=_params(("parallel", "arbitrary")),
        name="retention",
    )(p, p, p, p, cos, sin, d, xi, zeta, gt, ln_g.reshape(1, RET_HEADS * RET_DV))


def _sgu_kernel(u_ref, v_ref, lg_ref, lb_ref, ws_ref, bs_ref, o_ref, vn_ref, *, tm):
    v = v_ref[...]
    mu = jnp.mean(v, axis=-1, keepdims=True)
    vc = v - mu
    var = jnp.mean(vc * vc, axis=-1, keepdims=True)
    vn_ref[...] = (vc * lax.rsqrt(var + LN_EPS) * lg_ref[...] + lb_ref[...]).astype(BF16)
    ri = lax.broadcasted_iota(jnp.int32, (GM_BLOCK, GM_BLOCK), 0) // CHUNK
    ci = lax.broadcasted_iota(jnp.int32, (GM_BLOCK, GM_BLOCK), 1) // CHUNK
    gw = GM_WIDTH // GM_GROUPS
    for g in range(GM_GROUPS):
        wm = jnp.where(ci <= ri, ws_ref[g], 0.0).astype(BF16)
        cols = slice(g * gw, (g + 1) * gw)
        for n in range(tm // GM_BLOCK):
            rows = slice(n * GM_BLOCK, (n + 1) * GM_BLOCK)
            mixed = jnp.dot(wm, vn_ref[rows, cols], preferred_element_type=F32) + bs_ref[g]
            o_ref[rows, cols] = (u_ref[rows, cols] * mixed).astype(o_ref.dtype)


def spatial_gate(p, ln_g, ln_b, ws, bs, *, tm=512):
    m = p.shape[0]
    return pl.pallas_call(
        functools.partial(_sgu_kernel, tm=tm),
        grid=(m // tm,),
        in_specs=[pl.BlockSpec((tm, GM_WIDTH), lambda i: (i, 0)),
                  pl.BlockSpec((tm, GM_WIDTH), lambda i: (i, 1)),
                  pl.BlockSpec((1, GM_WIDTH), lambda i: (0, 0)),
                  pl.BlockSpec((1, GM_WIDTH), lambda i: (0, 0)),
                  pl.BlockSpec((GM_GROUPS, GM_BLOCK, GM_BLOCK), lambda i: (0, 0, 0)),
                  pl.BlockSpec((GM_GROUPS, GM_BLOCK, 1), lambda i: (0, 0, 0))],
        out_specs=pl.BlockSpec((tm, GM_WIDTH), lambda i: (i, 0)),
        out_shape=jax.ShapeDtypeStruct((m, GM_WIDTH), BF16),
        scratch_shapes=[pltpu.VMEM((tm, GM_WIDTH), BF16)],
        compiler_params=_params(("parallel",)),
        name="spatial_gate",
    )(p, p, ln_g.reshape(1, GM_WIDTH), ln_b.reshape(1, GM_WIDTH), ws,
      bs.reshape(GM_GROUPS, GM_BLOCK, 1))


def kernel(x, mix_norm_g, ffn_norm_g, final_norm_g, ev_w_in, ev_conv_dw_w, ev_conv_dw_b,
           ev_conv_ln_g, ev_conv_ln_b, ev_ret_ln_g, ev_w_out, od_w_in, od_gm_ln_g, od_gm_ln_b,
           od_gm_ws, od_gm_bs, od_w_out, ffn_w_up, ffn_dw_w, ffn_dw_b, ffn_w_down):
    bsz, s, d = x.shape
    xf = x.reshape(bsz * s, d)
    h = rms_norm_bf16(xf, mix_norm_g[0])
    for i in range(DEPTH):
        j = i // 2
        if i % 2 == 0:
            p = matmul(h, ev_w_in[j].astype(BF16), name="even_in")
            a_out = conformer_conv(p, ev_conv_dw_w[j], ev_conv_dw_b[j], ev_conv_ln_g[j],
                                   ev_conv_ln_b[j])
            b_out = retention(p, ev_ret_ln_g[j])
            z_list = [a_out, b_out]
            w_out = ev_w_out[j]
        else:
            p = matmul(h, od_w_in[j].astype(BF16), act="gelu", name="odd_in")
            z_list = [spatial_gate(p, od_gm_ln_g[j], od_gm_ln_b[j], od_gm_ws[j], od_gm_bs[j])]
            w_out = od_w_out[j]
        xf, h = matmul_residual_norm(z_list, w_out.astype(BF16), xf, ffn_norm_g[i], tm=512,
                                     tk=z_list[0].shape[1], name="mix_out")
        gated = ffn_up_gate(h, ffn_w_up[i].astype(BF16), ffn_dw_w[i], ffn_dw_b[i])
        w_down = ffn_w_down[i].astype(BF16)
        if i < DEPTH - 1:
            xf, h = matmul_residual_norm([gated], w_down, xf, mix_norm_g[i + 1], name="ffn_down")
        else:
            h = matmul_residual_norm([gated], w_down, xf, final_norm_g, h_dtype=F32,
                                     emit_x=False, name="ffn_down")
    return h.reshape(bsz, s, d)
```

```python
import functools

import jax
import jax.numpy as jnp
import numpy as np
from jax import lax
from jax.experimental import pallas as pl
from jax.experimental.pallas import tpu as pltpu

D_MODEL = 2048
SEQ = 2048
DEPTH = 4
CHUNK = 64
CONV_CH = 1024
CONV_K = 31
RET_HEADS = 4
RET_DK = 256
RET_DV = 256
GM_WIDTH = 2048
GM_BLOCK = 128
GM_GROUPS = 4
D_FF = 5632
FFN_K = 3
ROPE_THETA = 10000.0
RMS_EPS = 1e-6
LN_EPS = 1e-5

RET_BLOCK = 256
RET_HEADS_PER_STEP = 2
CONV_HALO = 32
LANES = 128

VMEM_LIMIT = 56 * 1024 * 1024

BF16 = jnp.bfloat16
F32 = jnp.float32


def _params(sem):
    return pltpu.CompilerParams(dimension_semantics=sem, vmem_limit_bytes=VMEM_LIMIT)


def _rms_kernel(x_ref, g_ref, o_ref):
    x = x_ref[...]
    ms = jnp.mean(x * x, axis=-1, keepdims=True)
    o_ref[...] = (x * lax.rsqrt(ms + RMS_EPS) * g_ref[...]).astype(o_ref.dtype)


def rms_norm_bf16(x, g, tm=1024):
    m, d = x.shape
    return pl.pallas_call(
        _rms_kernel,
        grid=(m // tm,),
        in_specs=[pl.BlockSpec((tm, d), lambda i: (i, 0)),
                  pl.BlockSpec((1, d), lambda i: (0, 0))],
        out_specs=pl.BlockSpec((tm, d), lambda i: (i, 0)),
        out_shape=jax.ShapeDtypeStruct((m, d), BF16),
        compiler_params=_params(("parallel",)),
        name="rms_norm",
    )(x, g.reshape(1, d))


def _mm_kernel(a_ref, w_ref, o_ref, *, act):
    acc = jnp.dot(a_ref[...], w_ref[...], preferred_element_type=F32)
    if act == "gelu":
        acc = 0.5 * acc * (1.0 + lax.erf(acc * np.float32(np.sqrt(0.5))))
    o_ref[...] = acc.astype(o_ref.dtype)


def matmul(a, w, *, act=None, out_dtype=F32, tm=1024, tn=1024, name="proj"):
    m, k = a.shape
    n = w.shape[1]
    return pl.pallas_call(
        functools.partial(_mm_kernel, act=act),
        grid=(m // tm, n // tn),
        in_specs=[pl.BlockSpec((tm, k), lambda i, j: (i, 0)),
                  pl.BlockSpec((k, tn), lambda i, j: (0, j))],
        out_specs=pl.BlockSpec((tm, tn), lambda i, j: (i, j)),
        out_shape=jax.ShapeDtypeStruct((m, n), out_dtype),
        compiler_params=_params(("parallel", "arbitrary")),
        name=name,
    )(a, w)


def _mm_res_kernel(*refs, n_a, nk, tm, rc, emit_x):
    a_refs, w_refs = refs[:n_a], refs[n_a:2 * n_a]
    x_ref, g_ref = refs[2 * n_a:2 * n_a + 2]
    if emit_x:
        xo_ref, ho_ref = refs[2 * n_a + 2:]
        acc_ref = xo_ref
    else:
        ho_ref, acc_ref = refs[2 * n_a + 2:]
    k = pl.program_id(1)
    xc = tm // nk

    def partial_product(rows):
        out = None
        for a_ref, w_ref in zip(a_refs, w_refs):
            p = jnp.dot(a_ref[rows, :], w_ref[...], preferred_element_type=F32)
            out = p if out is None else out + p
        return out

    def finish():
        for r in range(tm // rc):
            rows = slice(r * rc, (r + 1) * rc)
            xn = partial_product(rows)
            if nk == 1:
                xn = xn + x_ref[rows, :]
            else:
                xn = xn + acc_ref[rows, :]
                if r == nk - 1:
                    xn = xn + x_ref[...]
            if emit_x:
                xo_ref[rows, :] = xn
            ms = jnp.mean(xn * xn, axis=-1, keepdims=True)
            ho_ref[rows, :] = (xn * lax.rsqrt(ms + RMS_EPS) * g_ref[...]).astype(ho_ref.dtype)

    if nk == 1:
        finish()
        return

    @pl.when(k == 0)
    def _():
        acc_ref[...] = partial_product(slice(None))
        acc_ref[0:xc, :] += x_ref[...]

    @pl.when(jnp.logical_and(k > 0, k < nk - 1))
    def _():
        acc_ref[...] += partial_product(slice(None))
        acc_ref[pl.ds(pl.multiple_of(k * xc, xc), xc), :] += x_ref[...]

    @pl.when(k == nk - 1)
    def _():
        finish()


def matmul_residual_norm(a_list, w, x, g, *, h_dtype=BF16, emit_x=True, tm=1024, tk=512, rc=256,
                         name="proj_res"):
    n_a = len(a_list)
    m, ka = a_list[0].shape
    d = w.shape[1]
    nk = ka // tk
    assert ka == nk * tk and tm % nk == 0 and (nk == 1 or tm // nk == rc)
    a_specs = [pl.BlockSpec((tm, tk), lambda i, k: (i, k)) for _ in range(n_a)]
    w_specs = [pl.BlockSpec((tk, d), lambda i, k, q=q: (q * nk + k, 0)) for q in range(n_a)]
    row_spec = pl.BlockSpec((tm, d), lambda i, k: (i, 0))
    x_spec = pl.BlockSpec((tm // nk, d), lambda i, k: (i * nk + k, 0))
    if emit_x:
        out_specs = [row_spec, row_spec]
        out_shape = [jax.ShapeDtypeStruct((m, d), F32), jax.ShapeDtypeStruct((m, d), h_dtype)]
        scratch = []
    else:
        out_specs = row_spec
        out_shape = jax.ShapeDtypeStruct((m, d), h_dtype)
        scratch = [pltpu.VMEM((tm, d), F32)]
    return pl.pallas_call(
        functools.partial(_mm_res_kernel, n_a=n_a, nk=nk, tm=tm, rc=rc, emit_x=emit_x),
        grid=(m // tm, nk),
        in_specs=a_specs + w_specs + [x_spec, pl.BlockSpec((1, d), lambda i, k: (0, 0))],
        out_specs=out_specs,
        out_shape=out_shape,
        scratch_shapes=scratch,
        compiler_params=_params(("parallel", "arbitrary")),
        name=name,
    )(*a_list, *([w] * n_a), x, g.reshape(1, d))


FFN_PAD = 8


def _ffn_up_kernel(h_ref, wa_ref, wb_ref, cwa_ref, cwb_ref, cba_ref, cbb_ref, o_ref, s_ref,
                   *, rc, tn):
    nt = tn // LANES
    w_refs = (wa_ref, wb_ref)
    cw = (cwa_ref[...], cwb_ref[...])
    cb = (cba_ref[...], cbb_ref[...])
    for ab in range(2):
        for c in range(nt):
            s_ref[ab, c, 0:FFN_PAD, :] = jnp.zeros((FFN_PAD, LANES), F32)
    for r in range(SEQ // rc):
        r0 = r * rc
        hr = h_ref[r0:r0 + rc, :]
        u = [jnp.dot(hr, w_refs[ab][...], preferred_element_type=F32) for ab in range(2)]
        for ab in range(2):
            for c in range(nt):
                s_ref[ab, c, FFN_PAD + r0:FFN_PAD + r0 + rc, :] = u[ab][:, c * LANES:(c + 1) * LANES]
        for c in range(nt):
            lanes = slice(c * LANES, (c + 1) * LANES)
            y = []
            for ab in range(2):
                u1 = s_ref[ab, c, FFN_PAD + r0 - 1:FFN_PAD + r0 - 1 + rc, :]
                u2 = s_ref[ab, c, FFN_PAD + r0 - 2:FFN_PAD + r0 - 2 + rc, :]
                y.append(cw[ab][2:3, lanes] * u[ab][:, lanes] + cw[ab][1:2, lanes] * u1
                         + cw[ab][0:1, lanes] * u2 + cb[ab][:, lanes])
            a, b = y
            o_ref[r0:r0 + rc, lanes] = (a * jax.nn.sigmoid(a) * b).astype(o_ref.dtype)


def ffn_up_gate(h, w_up, dw_w, dw_b, *, tn=512, rc=512):
    m, k = h.shape
    nj = D_FF // tn
    dw_b = dw_b.reshape(1, 2 * D_FF)
    return pl.pallas_call(
        functools.partial(_ffn_up_kernel, rc=rc, tn=tn),
        grid=(m // SEQ, nj),
        in_specs=[pl.BlockSpec((SEQ, k), lambda i, j: (i, 0)),
                  pl.BlockSpec((k, tn), lambda i, j: (0, j)),
                  pl.BlockSpec((k, tn), lambda i, j: (0, j + nj)),
                  pl.BlockSpec((FFN_K, tn), lambda i, j: (0, j)),
                  pl.BlockSpec((FFN_K, tn), lambda i, j: (0, j + nj)),
                  pl.BlockSpec((1, tn), lambda i, j: (0, j)),
                  pl.BlockSpec((1, tn), lambda i, j: (0, j + nj))],
        out_specs=pl.BlockSpec((SEQ, tn), lambda i, j: (i, j)),
        out_shape=jax.ShapeDtypeStruct((m, D_FF), BF16),
        scratch_shapes=[pltpu.VMEM((2, tn // LANES, FFN_PAD + SEQ, LANES), F32)],
        compiler_params=_params(("parallel", "arbitrary")),
        name="ffn_up_gate",
    )(h, w_up, w_up, dw_w, dw_w, dw_b, dw_b)


def _conformer_out_kernel(ga_ref, gb_ref, ha_ref, hb_ref, w_ref, b_ref, lg_ref, lb_ref, r_ref,
                          wa_ref, wb_ref, x_ref, g_ref, xo_ref, ho_ref, u_ref, y_ref, a_ref,
                          *, ts, rc, tiles_per_seq):
    i = pl.program_id(0)
    nc = CONV_CH // LANES
    not_first = (i % tiles_per_seq != 0).astype(F32)
    uh = ha_ref[...] * jax.nn.sigmoid(hb_ref[...]) * not_first
    u = ga_ref[...] * jax.nn.sigmoid(gb_ref[...])
    for c in range(nc):
        lanes = slice(c * LANES, (c + 1) * LANES)
        u_ref[c, 0:CONV_HALO, :] = uh[:, lanes]
        u_ref[c, CONV_HALO:CONV_HALO + ts, :] = u[:, lanes]
    base = CONV_HALO - (CONV_K - 1)
    sub = 64
    for r in range(ts // rc):
        for c in range(nc):
            lanes = slice(c * LANES, (c + 1) * LANES)
            for q in range(rc // sub):
                r0 = r * rc + q * sub
                acc = jnp.broadcast_to(b_ref[:, lanes], (sub, LANES))
                for k in range(CONV_K):
                    acc = acc + w_ref[k:k + 1, lanes] * u_ref[c, r0 + base + k:r0 + base + k + sub, :]
                y_ref[r0:r0 + sub, lanes] = acc
        rows = slice(r * rc, (r + 1) * rc)
        y = y_ref[rows, :]
        mu = jnp.mean(y, axis=-1, keepdims=True)
        yc = y - mu
        var = jnp.mean(yc * yc, axis=-1, keepdims=True)
        yn = yc * lax.rsqrt(var + LN_EPS) * lg_ref[...] + lb_ref[...]
        a_ref[rows, :] = (yn * jax.nn.sigmoid(yn)).astype(BF16)
        xn = (x_ref[rows, :]
              + jnp.dot(a_ref[rows, :], wa_ref[...], preferred_element_type=F32)
              + jnp.dot(r_ref[rows, :], wb_ref[...], preferred_element_type=F32))
        _rms_finish(xn, g_ref, xo_ref, ho_ref, rows)


def conformer_out(p, w_dw, b_dw, ln_g, ln_b, ret_out, w_out, x, g, *, ts=512, rc=256):
    m = p.shape[0]
    d = w_out.shape[1]
    tiles_per_seq = SEQ // ts
    halo_per_tile = ts // CONV_HALO
    row = lambda i: (i, 0)
    fixed2 = lambda i: (0, 0)

    def halo_idx(col):
        return lambda i: (jnp.maximum(i * halo_per_tile - 1, 0), col)

    return pl.pallas_call(
        functools.partial(_conformer_out_kernel, ts=ts, rc=rc, tiles_per_seq=tiles_per_seq),
        grid=(m // ts,),
        in_specs=[pl.BlockSpec((ts, CONV_CH), row),
                  pl.BlockSpec((ts, CONV_CH), lambda i: (i, 1)),
                  pl.BlockSpec((CONV_HALO, CONV_CH), halo_idx(0)),
                  pl.BlockSpec((CONV_HALO, CONV_CH), halo_idx(1)),
                  pl.BlockSpec((CONV_K, CONV_CH), fixed2),
                  pl.BlockSpec((1, CONV_CH), fixed2),
                  pl.BlockSpec((1, CONV_CH), fixed2),
                  pl.BlockSpec((1, CONV_CH), fixed2),
                  pl.BlockSpec((ts, CONV_CH), row),
                  pl.BlockSpec((CONV_CH, d), fixed2, pipeline_mode=pl.Buffered(1)),
                  pl.BlockSpec((CONV_CH, d), lambda i: (1, 0), pipeline_mode=pl.Buffered(1)),
                  pl.BlockSpec((ts, d), row),
                  pl.BlockSpec((1, d), fixed2)],
        out_specs=[pl.BlockSpec((ts, d), row), pl.BlockSpec((ts, d), row)],
        out_shape=[jax.ShapeDtypeStruct((m, d), F32), jax.ShapeDtypeStruct((m, d), BF16)],
        scratch_shapes=[pltpu.VMEM((CONV_CH // LANES, CONV_HALO + ts, LANES), F32),
                        pltpu.VMEM((ts, CONV_CH), F32),
                        pltpu.VMEM((ts, CONV_CH), BF16)],
        compiler_params=_params(("parallel",)),
        name="conformer_out",
    )(p, p, p, p, w_dw, b_dw.reshape(1, CONV_CH), ln_g.reshape(1, CONV_CH),
      ln_b.reshape(1, CONV_CH), ret_out, w_out, w_out, x, g.reshape(1, d))


def _retention_tables():
    t = RET_BLOCK
    log_gamma = jnp.log1p(-jnp.exp2(-5.0 - jnp.arange(RET_HEADS, dtype=F32)))
    pos = jnp.arange(t, dtype=F32)
    dist = jnp.abs(pos[:, None] - pos[None, :])
    chunk = jnp.arange(t) // CHUNK
    mask = chunk[None, :] <= chunk[:, None]
    d = jnp.where(mask[None], jnp.exp(dist[None] * log_gamma[:, None, None]), 0.0)
    xi = jnp.exp((pos + 1.0)[None] * log_gamma[:, None])
    zeta = jnp.exp((t - 1.0 - pos)[None] * log_gamma[:, None])
    gt = jnp.exp(t * log_gamma)
    xi = jnp.broadcast_to(xi[:, :, None], (RET_HEADS, t, RET_DV))
    zeta = jnp.broadcast_to(zeta[:, :, None], (RET_HEADS, t, RET_DK))
    gt = jnp.broadcast_to(gt[:, None, None], (RET_HEADS, 1, RET_DV))
    inv = ROPE_THETA ** (-jnp.arange(0, RET_DK, 2, dtype=F32) / RET_DK)
    ang = jnp.arange(SEQ, dtype=F32)[:, None] * inv[None, :]
    return d, xi, zeta, gt, jnp.cos(ang), jnp.sin(ang)


def _rotary(x, cos, sin):
    half = x.shape[-1] // 2
    x1, x2 = x[:, :half], x[:, half:]
    return jnp.concatenate([x1 * cos - x2 * sin, x1 * sin + x2 * cos], axis=-1)


def _retention_kernel(q_ref, k_ref, v_ref, gate_ref, cos_ref, sin_ref, d_ref, xi_ref, zeta_ref,
                      gt_ref, lg_ref, o_ref, state_ref):
    t = RET_BLOCK
    state_ref[...] = jnp.zeros_like(state_ref)
    scale = np.float32(RET_DK ** -0.5)

    def body(blk, carry):
        r0 = pl.multiple_of(blk * t, t)
        rows = pl.ds(r0, t)
        cos = cos_ref[rows, :]
        sin = sin_ref[rows, :]
        for hd in range(RET_HEADS_PER_STEP):
            kc = slice(hd * RET_DK, (hd + 1) * RET_DK)
            vc = slice(hd * RET_DV, (hd + 1) * RET_DV)
            q = _rotary(q_ref[rows, kc], cos, sin)
            k = _rotary(k_ref[rows, kc], cos, sin) * scale
            v = v_ref[rows, vc].astype(BF16)
            qb = q.astype(BF16)
            s = lax.dot_general(qb, k.astype(BF16), (((1,), (1,)), ((), ())),
                                preferred_element_type=F32) * d_ref[hd]
            state = state_ref[hd]
            o = jnp.dot(s.astype(BF16), v, preferred_element_type=F32)
            o = o + xi_ref[hd] * jnp.dot(qb, state.astype(BF16), preferred_element_type=F32)
            kz = (k * zeta_ref[hd]).astype(BF16)
            state_ref[hd] = gt_ref[hd] * state + lax.dot_general(
                kz, v, (((0,), (0,)), ((), ())), preferred_element_type=F32)
            mu = jnp.mean(o, axis=-1, keepdims=True)
            oc = o - mu
            var = jnp.mean(oc * oc, axis=-1, keepdims=True)
            on = oc * lax.rsqrt(var + LN_EPS) * lg_ref[:, vc]
            g = gate_ref[rows, vc]
            o_ref[rows, vc] = (on * (g * jax.nn.sigmoid(g))).astype(o_ref.dtype)
        return carry

    lax.fori_loop(0, SEQ // t, body, 0)


def retention(p, ln_g):
    m = p.shape[0]
    d, xi, zeta, gt, cos, sin = _retention_tables()
    t = RET_BLOCK
    hp = RET_HEADS_PER_STEP
    groups = RET_HEADS // hp
    col0 = 2 * CONV_CH // (hp * RET_DK)

    def col(off):
        return lambda b, h: (b, col0 + off * groups + h)

    return pl.pallas_call(
        _retention_kernel,
        grid=(m // SEQ, groups),
        in_specs=[pl.BlockSpec((SEQ, hp * RET_DK), col(0)),
                  pl.BlockSpec((SEQ, hp * RET_DK), col(1)),
                  pl.BlockSpec((SEQ, hp * RET_DV), col(2)),
                  pl.BlockSpec((SEQ, hp * RET_DV), col(3)),
                  pl.BlockSpec((SEQ, RET_DK // 2), lambda b, h: (0, 0)),
                  pl.BlockSpec((SEQ, RET_DK // 2), lambda b, h: (0, 0)),
                  pl.BlockSpec((hp, t, t), lambda b, h: (h, 0, 0)),
                  pl.BlockSpec((hp, t, RET_DV), lambda b, h: (h, 0, 0)),
                  pl.BlockSpec((hp, t, RET_DK), lambda b, h: (h, 0, 0)),
                  pl.BlockSpec((hp, 1, RET_DV), lambda b, h: (h, 0, 0)),
                  pl.BlockSpec((1, hp * RET_DV), lambda b, h: (0, h))],
        out_specs=pl.BlockSpec((SEQ, hp * RET_DV), lambda b, h: (b, h)),
        out_shape=jax.ShapeDtypeStruct((m, RET_HEADS * RET_DV), BF16),
        scratch_shapes=[pltpu.VMEM((hp, RET_DK, RET_DV), F32)],
        compiler_params=_params(("parallel", "arbitrary")),
        name="retention",
    )(p, p, p, p, cos, sin, d, xi, zeta, gt, ln_g.reshape(1, RET_HEADS * RET_DV))


def _rms_finish(xn, g_ref, xo_ref, ho_ref, rows):
    xo_ref[rows, :] = xn
    ms = jnp.mean(xn * xn, axis=-1, keepdims=True)
    ho_ref[rows, :] = (xn * lax.rsqrt(ms + RMS_EPS) * g_ref[...]).astype(ho_ref.dtype)


def _sgu_out_kernel(u_ref, v_ref, lg_ref, lb_ref, ws_ref, bs_ref, w_ref, x_ref, g_ref,
                    xo_ref, ho_ref, vn_ref, z_ref, *, tm, rc):
    ri = lax.broadcasted_iota(jnp.int32, (GM_BLOCK, GM_BLOCK), 0) // CHUNK
    ci = lax.broadcasted_iota(jnp.int32, (GM_BLOCK, GM_BLOCK), 1) // CHUNK
    gw = GM_WIDTH // GM_GROUPS
    wm = [jnp.where(ci <= ri, ws_ref[g], 0.0).astype(BF16) for g in range(GM_GROUPS)]
    for r in range(tm // rc):
        for n in range(r * rc // GM_BLOCK, (r + 1) * rc // GM_BLOCK):
            blk = slice(n * GM_BLOCK, (n + 1) * GM_BLOCK)
            v = v_ref[blk, :]
            mu = jnp.mean(v, axis=-1, keepdims=True)
            vc = v - mu
            var = jnp.mean(vc * vc, axis=-1, keepdims=True)
            vn_ref[blk, :] = (vc * lax.rsqrt(var + LN_EPS) * lg_ref[...] + lb_ref[...]).astype(BF16)
            for g in range(GM_GROUPS):
                cols = slice(g * gw, (g + 1) * gw)
                mixed = jnp.dot(wm[g], vn_ref[blk, cols], preferred_element_type=F32) + bs_ref[g]
                z_ref[blk, cols] = (u_ref[blk, cols] * mixed).astype(BF16)
        rows = slice(r * rc, (r + 1) * rc)
        xn = x_ref[rows, :] + jnp.dot(z_ref[rows, :], w_ref[...], preferred_element_type=F32)
        _rms_finish(xn, g_ref, xo_ref, ho_ref, rows)


def spatial_gate_out(p, ln_g, ln_b, ws, bs, w_out, x, g, *, tm=512, rc=256):
    m = p.shape[0]
    d = w_out.shape[1]
    row = lambda i: (i, 0)
    fixed2 = lambda i: (0, 0)
    return pl.pallas_call(
        functools.partial(_sgu_out_kernel, tm=tm, rc=rc),
        grid=(m // tm,),
        in_specs=[pl.BlockSpec((tm, GM_WIDTH), row),
                  pl.BlockSpec((tm, GM_WIDTH), lambda i: (i, 1)),
                  pl.BlockSpec((1, GM_WIDTH), fixed2),
                  pl.BlockSpec((1, GM_WIDTH), fixed2),
                  pl.BlockSpec((GM_GROUPS, GM_BLOCK, GM_BLOCK), lambda i: (0, 0, 0)),
                  pl.BlockSpec((GM_GROUPS, GM_BLOCK, 1), lambda i: (0, 0, 0)),
                  pl.BlockSpec((GM_WIDTH, d), fixed2, pipeline_mode=pl.Buffered(1)),
                  pl.BlockSpec((tm, d), row),
                  pl.BlockSpec((1, d), fixed2)],
        out_specs=[pl.BlockSpec((tm, d), row), pl.BlockSpec((tm, d), row)],
        out_shape=[jax.ShapeDtypeStruct((m, d), F32), jax.ShapeDtypeStruct((m, d), BF16)],
        scratch_shapes=[pltpu.VMEM((tm, GM_WIDTH), BF16), pltpu.VMEM((tm, GM_WIDTH), BF16)],
        compiler_params=_params(("parallel",)),
        name="spatial_gate_out",
    )(p, p, ln_g.reshape(1, GM_WIDTH), ln_b.reshape(1, GM_WIDTH), ws,
      bs.reshape(GM_GROUPS, GM_BLOCK, 1), w_out, x, g.reshape(1, d))


def kernel(x, mix_norm_g, ffn_norm_g, final_norm_g, ev_w_in, ev_conv_dw_w, ev_conv_dw_b,
           ev_conv_ln_g, ev_conv_ln_b, ev_ret_ln_g, ev_w_out, od_w_in, od_gm_ln_g, od_gm_ln_b,
           od_gm_ws, od_gm_bs, od_w_out, ffn_w_up, ffn_dw_w, ffn_dw_b, ffn_w_down):
    bsz, s, d = x.shape
    xf = x.reshape(bsz * s, d)
    h = rms_norm_bf16(xf, mix_norm_g[0])
    for i in range(DEPTH):
        j = i // 2
        if i % 2 == 0:
            p = matmul(h, ev_w_in[j].astype(BF16), name="even_in")
            b_out = retention(p, ev_ret_ln_g[j])
            xf, h = conformer_out(p, ev_conv_dw_w[j], ev_conv_dw_b[j], ev_conv_ln_g[j],
                                  ev_conv_ln_b[j], b_out, ev_w_out[j].astype(BF16), xf,
                                  ffn_norm_g[i])
        else:
            p = matmul(h, od_w_in[j].astype(BF16), act="gelu", name="odd_in")
            xf, h = spatial_gate_out(p, od_gm_ln_g[j], od_gm_ln_b[j], od_gm_ws[j], od_gm_bs[j],
                                     od_w_out[j].astype(BF16), xf, ffn_norm_g[i])
        gated = ffn_up_gate(h, ffn_w_up[i].astype(BF16), ffn_dw_w[i], ffn_dw_b[i])
        w_down = ffn_w_down[i].astype(BF16)
        if i < DEPTH - 1:
            xf, h = matmul_residual_norm([gated], w_down, xf, mix_norm_g[i + 1], tm=512,
                                         tk=D_FF // 2, name="ffn_down")
        else:
            h = matmul_residual_norm([gated], w_down, xf, final_norm_g, h_dtype=F32,
                                     emit_x=False, tm=512, tk=D_FF // 2, name="ffn_down")
    return h.reshape(bsz, s, d)
```

```python
import functools

import jax
import jax.numpy as jnp
import numpy as np
from jax import lax
from jax.experimental import pallas as pl
from jax.experimental.pallas import tpu as pltpu

D_MODEL = 2048
SEQ = 2048
DEPTH = 4
CHUNK = 64
CONV_CH = 1024
CONV_K = 31
RET_HEADS = 4
RET_DK = 256
RET_DV = 256
GM_WIDTH = 2048
GM_BLOCK = 128
GM_GROUPS = 4
D_FF = 5632
FFN_K = 3
ROPE_THETA = 10000.0
RMS_EPS = 1e-6
LN_EPS = 1e-5

RET_BLOCK = 256
RET_HEADS_PER_STEP = 2
CONV_HALO = 32
LANES = 128

VMEM_LIMIT = 56 * 1024 * 1024

BF16 = jnp.bfloat16
F32 = jnp.float32


def _params(sem):
    return pltpu.CompilerParams(dimension_semantics=sem, vmem_limit_bytes=VMEM_LIMIT)


def _rms_kernel(x_ref, g_ref, o_ref):
    x = x_ref[...]
    ms = jnp.mean(x * x, axis=-1, keepdims=True)
    o_ref[...] = (x * lax.rsqrt(ms + RMS_EPS) * g_ref[...]).astype(o_ref.dtype)


def rms_norm_bf16(x, g, tm=1024):
    m, d = x.shape
    return pl.pallas_call(
        _rms_kernel,
        grid=(m // tm,),
        in_specs=[pl.BlockSpec((tm, d), lambda i: (i, 0)),
                  pl.BlockSpec((1, d), lambda i: (0, 0))],
        out_specs=pl.BlockSpec((tm, d), lambda i: (i, 0)),
        out_shape=jax.ShapeDtypeStruct((m, d), BF16),
        compiler_params=_params(("parallel",)),
        name="rms_norm",
    )(x, g.reshape(1, d))


def _mm_kernel(a_ref, w_ref, o_ref, *, act):
    acc = jnp.dot(a_ref[...], w_ref[...], preferred_element_type=F32)
    if act == "gelu":
        acc = 0.5 * acc * (1.0 + lax.erf(acc * np.float32(np.sqrt(0.5))))
    o_ref[...] = acc.astype(o_ref.dtype)


def matmul(a, w, *, act=None, out_dtype=F32, tm=1024, tn=1024, name="proj"):
    m, k = a.shape
    n = w.shape[1]
    return pl.pallas_call(
        functools.partial(_mm_kernel, act=act),
        grid=(m // tm, n // tn),
        in_specs=[pl.BlockSpec((tm, k), lambda i, j: (i, 0)),
                  pl.BlockSpec((k, tn), lambda i, j: (0, j))],
        out_specs=pl.BlockSpec((tm, tn), lambda i, j: (i, j)),
        out_shape=jax.ShapeDtypeStruct((m, n), out_dtype),
        compiler_params=_params(("parallel", "arbitrary")),
        name=name,
    )(a, w)


def _rms_finish(xn, g_ref, xo_ref, ho_ref, rows):
    if xo_ref is not None:
        xo_ref[rows, :] = xn
    ms = jnp.mean(xn * xn, axis=-1, keepdims=True)
    ho_ref[rows, :] = (xn * lax.rsqrt(ms + RMS_EPS) * g_ref[...]).astype(ho_ref.dtype)


def _mm_res_kernel(a_ref, w_ref, x_ref, g_ref, *out_refs, tm, rc):
    ho_ref = out_refs[-1]
    xo_ref = out_refs[0] if len(out_refs) == 2 else None
    for r in range(tm // rc):
        rows = slice(r * rc, (r + 1) * rc)
        xn = x_ref[rows, :] + jnp.dot(a_ref[rows, :], w_ref[...], preferred_element_type=F32)
        _rms_finish(xn, g_ref, xo_ref, ho_ref, rows)


def matmul_residual_norm(a, w, x, g, *, h_dtype=BF16, emit_x=True, tm=512, rc=256,
                         name="proj_res"):
    m, ka = a.shape
    d = w.shape[1]
    row = lambda i: (i, 0)
    fixed = lambda i: (0, 0)
    h_shape = jax.ShapeDtypeStruct((m, d), h_dtype)
    if emit_x:
        out_specs = [pl.BlockSpec((tm, d), row), pl.BlockSpec((tm, d), row)]
        out_shape = [jax.ShapeDtypeStruct((m, d), F32), h_shape]
    else:
        out_specs = pl.BlockSpec((tm, d), row)
        out_shape = h_shape
    return pl.pallas_call(
        functools.partial(_mm_res_kernel, tm=tm, rc=rc),
        grid=(m // tm,),
        in_specs=[pl.BlockSpec((tm, ka), row),
                  pl.BlockSpec((ka, d), fixed, pipeline_mode=pl.Buffered(1)),
                  pl.BlockSpec((tm, d), row),
                  pl.BlockSpec((1, d), fixed)],
        out_specs=out_specs,
        out_shape=out_shape,
        compiler_params=_params(("parallel",)),
        name=name,
    )(a, w, x, g.reshape(1, d))


FFN_PAD = 8
def _ffn_up_kernel(h_ref, wa_ref, wb_ref, cwa_ref, cwb_ref, cba_ref, cbb_ref, o_ref, s_ref,
                   *, rc, tn):
    nt = tn // LANES
    w_refs = (wa_ref, wb_ref)
    cw = (cwa_ref[...], cwb_ref[...])
    cb = (cba_ref[...], cbb_ref[...])
    for ab in range(2):
        for c in range(nt):
            s_ref[ab, c, 0:FFN_PAD, :] = jnp.zeros((FFN_PAD, LANES), F32)
    for r in range(SEQ // rc):
        r0 = r * rc
        hr = h_ref[r0:r0 + rc, :]
        u = [jnp.dot(hr, w_refs[ab][...], preferred_element_type=F32) for ab in range(2)]
        for ab in range(2):
            for c in range(nt):
                s_ref[ab, c, FFN_PAD + r0:FFN_PAD + r0 + rc, :] = u[ab][:, c * LANES:(c + 1) * LANES]
        for c in range(nt):
            lanes = slice(c * LANES, (c + 1) * LANES)
            y = []
            for ab in range(2):
                u1 = s_ref[ab, c, FFN_PAD + r0 - 1:FFN_PAD + r0 - 1 + rc, :]
                u2 = s_ref[ab, c, FFN_PAD + r0 - 2:FFN_PAD + r0 - 2 + rc, :]
                y.append(cw[ab][2:3, lanes] * u[ab][:, lanes] + cw[ab][1:2, lanes] * u1
                         + cw[ab][0:1, lanes] * u2 + cb[ab][:, lanes])
            a, b = y
            o_ref[r0:r0 + rc, lanes] = (a * jax.nn.sigmoid(a) * b).astype(o_ref.dtype)


def ffn_up_gate(h, w_up, dw_w, dw_b, *, tn=512, rc=1024):
    m, k = h.shape
    nj = D_FF // tn
    dw_b = dw_b.reshape(1, 2 * D_FF)
    return pl.pallas_call(
        functools.partial(_ffn_up_kernel, rc=rc, tn=tn),
        grid=(m // SEQ, nj),
        in_specs=[pl.BlockSpec((SEQ, k), lambda i, j: (i, 0)),
                  pl.BlockSpec((k, tn), lambda i, j: (0, j)),
                  pl.BlockSpec((k, tn), lambda i, j: (0, j + nj)),
                  pl.BlockSpec((FFN_K, tn), lambda i, j: (0, j)),
                  pl.BlockSpec((FFN_K, tn), lambda i, j: (0, j + nj)),
                  pl.BlockSpec((1, tn), lambda i, j: (0, j)),
                  pl.BlockSpec((1, tn), lambda i, j: (0, j + nj))],
        out_specs=pl.BlockSpec((SEQ, tn), lambda i, j: (i, j)),
        out_shape=jax.ShapeDtypeStruct((m, D_FF), BF16),
        scratch_shapes=[pltpu.VMEM((2, tn // LANES, FFN_PAD + SEQ, LANES), F32)],
        compiler_params=_params(("parallel", "arbitrary")),
        name="ffn_up_gate",
    )(h, w_up, w_up, dw_w, dw_w, dw_b, dw_b)


def _conformer_out_kernel(ga_ref, gb_ref, ha_ref, hb_ref, w_ref, b_ref, lg_ref, lb_ref, r_ref,
                          wa_ref, wb_ref, x_ref, g_ref, xo_ref, ho_ref, u_ref, y_ref, a_ref,
                          *, ts, rc, tiles_per_seq):
    i = pl.program_id(0)
    nc = CONV_CH // LANES
    not_first = (i % tiles_per_seq != 0).astype(F32)
    uh = ha_ref[...] * jax.nn.sigmoid(hb_ref[...]) * not_first
    u = ga_ref[...] * jax.nn.sigmoid(gb_ref[...])
    for c in range(nc):
        lanes = slice(c * LANES, (c + 1) * LANES)
        u_ref[c, 0:CONV_HALO, :] = uh[:, lanes]
        u_ref[c, CONV_HALO:CONV_HALO + ts, :] = u[:, lanes]
    base = CONV_HALO - (CONV_K - 1)
    sub = 64
    for r in range(ts // rc):
        for c in range(nc):
            lanes = slice(c * LANES, (c + 1) * LANES)
            for q in range(rc // sub):
                r0 = r * rc + q * sub
                acc = jnp.broadcast_to(b_ref[:, lanes], (sub, LANES))
                for k in range(CONV_K):
                    acc = acc + w_ref[k:k + 1, lanes] * u_ref[c, r0 + base + k:r0 + base + k + sub, :]
                y_ref[r0:r0 + sub, lanes] = acc
        rows = slice(r * rc, (r + 1) * rc)
        y = y_ref[rows, :]
        mu = jnp.mean(y, axis=-1, keepdims=True)
        yc = y - mu
        var = jnp.mean(yc * yc, axis=-1, keepdims=True)
        yn = yc * lax.rsqrt(var + LN_EPS) * lg_ref[...] + lb_ref[...]
        a_ref[rows, :] = (yn * jax.nn.sigmoid(yn)).astype(BF16)
        xn = (x_ref[rows, :]
              + jnp.dot(a_ref[rows, :], wa_ref[...], preferred_element_type=F32)
              + jnp.dot(r_ref[rows, :], wb_ref[...], preferred_element_type=F32))
        _rms_finish(xn, g_ref, xo_ref, ho_ref, rows)


def conformer_out(p, w_dw, b_dw, ln_g, ln_b, ret_out, w_out, x, g, *, ts=512, rc=256):
    m = p.shape[0]
    d = w_out.shape[1]
    tiles_per_seq = SEQ // ts
    halo_per_tile = ts // CONV_HALO
    row = lambda i: (i, 0)
    fixed2 = lambda i: (0, 0)

    def halo_idx(col):
        return lambda i: (jnp.maximum(i * halo_per_tile - 1, 0), col)

    return pl.pallas_call(
        functools.partial(_conformer_out_kernel, ts=ts, rc=rc, tiles_per_seq=tiles_per_seq),
        grid=(m // ts,),
        in_specs=[pl.BlockSpec((ts, CONV_CH), row),
                  pl.BlockSpec((ts, CONV_CH), lambda i: (i, 1)),
                  pl.BlockSpec((CONV_HALO, CONV_CH), halo_idx(0)),
                  pl.BlockSpec((CONV_HALO, CONV_CH), halo_idx(1)),
                  pl.BlockSpec((CONV_K, CONV_CH), fixed2),
                  pl.BlockSpec((1, CONV_CH), fixed2),
                  pl.BlockSpec((1, CONV_CH), fixed2),
                  pl.BlockSpec((1, CONV_CH), fixed2),
                  pl.BlockSpec((ts, CONV_CH), row),
                  pl.BlockSpec((CONV_CH, d), fixed2, pipeline_mode=pl.Buffered(1)),
                  pl.BlockSpec((CONV_CH, d), lambda i: (1, 0), pipeline_mode=pl.Buffered(1)),
                  pl.BlockSpec((ts, d), row),
                  pl.BlockSpec((1, d), fixed2)],
        out_specs=[pl.BlockSpec((ts, d), row), pl.BlockSpec((ts, d), row)],
        out_shape=[jax.ShapeDtypeStruct((m, d), F32), jax.ShapeDtypeStruct((m, d), BF16)],
        scratch_shapes=[pltpu.VMEM((CONV_CH // LANES, CONV_HALO + ts, LANES), F32),
                        pltpu.VMEM((ts, CONV_CH), F32),
                        pltpu.VMEM((ts, CONV_CH), BF16)],
        compiler_params=_params(("parallel",)),
        name="conformer_out",
    )(p, p, p, p, w_dw, b_dw.reshape(1, CONV_CH), ln_g.reshape(1, CONV_CH),
      ln_b.reshape(1, CONV_CH), ret_out, w_out, w_out, x, g.reshape(1, d))


def _retention_tables():
    t = RET_BLOCK
    log_gamma = jnp.log1p(-jnp.exp2(-5.0 - jnp.arange(RET_HEADS, dtype=F32)))
    pos = jnp.arange(t, dtype=F32)
    dist = jnp.abs(pos[:, None] - pos[None, :])
    chunk = jnp.arange(t) // CHUNK
    mask = chunk[None, :] <= chunk[:, None]
    d = jnp.where(mask[None], jnp.exp(dist[None] * log_gamma[:, None, None]), 0.0)
    xi = jnp.exp((pos + 1.0)[None] * log_gamma[:, None])
    zeta = jnp.exp((t - 1.0 - pos)[None] * log_gamma[:, None])
    gt = jnp.exp(t * log_gamma)
    xi = jnp.broadcast_to(xi[:, :, None], (RET_HEADS, t, RET_DV))
    zeta = jnp.broadcast_to(zeta[:, :, None], (RET_HEADS, t, RET_DK))
    gt = jnp.broadcast_to(gt[:, None, None], (RET_HEADS, 1, RET_DV))
    inv = ROPE_THETA ** (-jnp.arange(0, RET_DK, 2, dtype=F32) / RET_DK)
    ang = jnp.arange(SEQ, dtype=F32)[:, None] * inv[None, :]
    return d, xi, zeta, gt, jnp.cos(ang), jnp.sin(ang)


def _rotary(x, cos, sin):
    half = x.shape[-1] // 2
    x1, x2 = x[:, :half], x[:, half:]
    return jnp.concatenate([x1 * cos - x2 * sin, x1 * sin + x2 * cos], axis=-1)


def _retention_kernel(q_ref, k_ref, v_ref, gate_ref, cos_ref, sin_ref, d_ref, xi_ref, zeta_ref,
                      gt_ref, lg_ref, o_ref, state_ref):
    t = RET_BLOCK
    state_ref[...] = jnp.zeros_like(state_ref)
    scale = np.float32(RET_DK ** -0.5)

    def body(blk, carry):
        r0 = pl.multiple_of(blk * t, t)
        rows = pl.ds(r0, t)
        cos = cos_ref[rows, :]
        sin = sin_ref[rows, :]
        for hd in range(RET_HEADS_PER_STEP):
            kc = slice(hd * RET_DK, (hd + 1) * RET_DK)
            vc = slice(hd * RET_DV, (hd + 1) * RET_DV)
            q = _rotary(q_ref[rows, kc], cos, sin)
            k = _rotary(k_ref[rows, kc], cos, sin) * scale
            v = v_ref[rows, vc].astype(BF16)
            qb = q.astype(BF16)
            s = lax.dot_general(qb, k.astype(BF16), (((1,), (1,)), ((), ())),
                                preferred_element_type=F32) * d_ref[hd]
            state = state_ref[hd]
            o = jnp.dot(s.astype(BF16), v, preferred_element_type=F32)
            o = o + xi_ref[hd] * jnp.dot(qb, state.astype(BF16), preferred_element_type=F32)
            kz = (k * zeta_ref[hd]).astype(BF16)
            state_ref[hd] = gt_ref[hd] * state + lax.dot_general(
                kz, v, (((0,), (0,)), ((), ())), preferred_element_type=F32)
            mu = jnp.mean(o, axis=-1, keepdims=True)
            oc = o - mu
            var = jnp.mean(oc * oc, axis=-1, keepdims=True)
            on = oc * lax.rsqrt(var + LN_EPS) * lg_ref[:, vc]
            g = gate_ref[rows, vc]
            o_ref[rows, vc] = (on * (g * jax.nn.sigmoid(g))).astype(o_ref.dtype)
        return carry

    lax.fori_loop(0, SEQ // t, body, 0)


def retention(p, ln_g):
    m = p.shape[0]
    d, xi, zeta, gt, cos, sin = _retention_tables()
    t = RET_BLOCK
    hp = RET_HEADS_PER_STEP
    groups = RET_HEADS // hp
    col0 = 2 * CONV_CH // (hp * RET_DK)

    def col(off):
        return lambda b, h: (b, col0 + off * groups + h)

    return pl.pallas_call(
        _retention_kernel,
        grid=(m // SEQ, groups),
        in_specs=[pl.BlockSpec((SEQ, hp * RET_DK), col(0)),
                  pl.BlockSpec((SEQ, hp * RET_DK), col(1)),
                  pl.BlockSpec((SEQ, hp * RET_DV), col(2)),
                  pl.BlockSpec((SEQ, hp * RET_DV), col(3)),
                  pl.BlockSpec((SEQ, RET_DK // 2), lambda b, h: (0, 0)),
                  pl.BlockSpec((SEQ, RET_DK // 2), lambda b, h: (0, 0)),
                  pl.BlockSpec((hp, t, t), lambda b, h: (h, 0, 0)),
                  pl.BlockSpec((hp, t, RET_DV), lambda b, h: (h, 0, 0)),
                  pl.BlockSpec((hp, t, RET_DK), lambda b, h: (h, 0, 0)),
                  pl.BlockSpec((hp, 1, RET_DV), lambda b, h: (h, 0, 0)),
                  pl.BlockSpec((1, hp * RET_DV), lambda b, h: (0, h))],
        out_specs=pl.BlockSpec((SEQ, hp * RET_DV), lambda b, h: (b, h)),
        out_shape=jax.ShapeDtypeStruct((m, RET_HEADS * RET_DV), BF16),
        scratch_shapes=[pltpu.VMEM((hp, RET_DK, RET_DV), F32)],
        compiler_params=_params(("parallel", "arbitrary")),
        name="retention",
    )(p, p, p, p, cos, sin, d, xi, zeta, gt, ln_g.reshape(1, RET_HEADS * RET_DV))


def _sgu_out_kernel(u_ref, v_ref, lg_ref, lb_ref, ws_ref, bs_ref, w_ref, x_ref, g_ref,
                    xo_ref, ho_ref, vn_ref, z_ref, *, tm, rc):
    ri = lax.broadcasted_iota(jnp.int32, (GM_BLOCK, GM_BLOCK), 0) // CHUNK
    ci = lax.broadcasted_iota(jnp.int32, (GM_BLOCK, GM_BLOCK), 1) // CHUNK
    gw = GM_WIDTH // GM_GROUPS
    wm = [jnp.where(ci <= ri, ws_ref[g], 0.0).astype(BF16) for g in range(GM_GROUPS)]
    for r in range(tm // rc):
        for n in range(r * rc // GM_BLOCK, (r + 1) * rc // GM_BLOCK):
            blk = slice(n * GM_BLOCK, (n + 1) * GM_BLOCK)
            v = v_ref[blk, :]
            mu = jnp.mean(v, axis=-1, keepdims=True)
            vc = v - mu
            var = jnp.mean(vc * vc, axis=-1, keepdims=True)
            vn_ref[blk, :] = (vc * lax.rsqrt(var + LN_EPS) * lg_ref[...] + lb_ref[...]).astype(BF16)
            for g in range(GM_GROUPS):
                cols = slice(g * gw, (g + 1) * gw)
                mixed = jnp.dot(wm[g], vn_ref[blk, cols], preferred_element_type=F32) + bs_ref[g]
                z_ref[blk, cols] = (u_ref[blk, cols] * mixed).astype(BF16)
        rows = slice(r * rc, (r + 1) * rc)
        xn = x_ref[rows, :] + jnp.dot(z_ref[rows, :], w_ref[...], preferred_element_type=F32)
        _rms_finish(xn, g_ref, xo_ref, ho_ref, rows)


def spatial_gate_out(p, ln_g, ln_b, ws, bs, w_out, x, g, *, tm=512, rc=256):
    m = p.shape[0]
    d = w_out.shape[1]
    row = lambda i: (i, 0)
    fixed2 = lambda i: (0, 0)
    return pl.pallas_call(
        functools.partial(_sgu_out_kernel, tm=tm, rc=rc),
        grid=(m // tm,),
        in_specs=[pl.BlockSpec((tm, GM_WIDTH), row),
                  pl.BlockSpec((tm, GM_WIDTH), lambda i: (i, 1)),
                  pl.BlockSpec((1, GM_WIDTH), fixed2),
                  pl.BlockSpec((1, GM_WIDTH), fixed2),
                  pl.BlockSpec((GM_GROUPS, GM_BLOCK, GM_BLOCK), lambda i: (0, 0, 0)),
                  pl.BlockSpec((GM_GROUPS, GM_BLOCK, 1), lambda i: (0, 0, 0)),
                  pl.BlockSpec((GM_WIDTH, d), fixed2, pipeline_mode=pl.Buffered(1)),
                  pl.BlockSpec((tm, d), row),
                  pl.BlockSpec((1, d), fixed2)],
        out_specs=[pl.BlockSpec((tm, d), row), pl.BlockSpec((tm, d), row)],
        out_shape=[jax.ShapeDtypeStruct((m, d), F32), jax.ShapeDtypeStruct((m, d), BF16)],
        scratch_shapes=[pltpu.VMEM((tm, GM_WIDTH), BF16), pltpu.VMEM((tm, GM_WIDTH), BF16)],
        compiler_params=_params(("parallel",)),
        name="spatial_gate_out",
    )(p, p, ln_g.reshape(1, GM_WIDTH), ln_b.reshape(1, GM_WIDTH), ws,
      bs.reshape(GM_GROUPS, GM_BLOCK, 1), w_out, x, g.reshape(1, d))


def kernel(x, mix_norm_g, ffn_norm_g, final_norm_g, ev_w_in, ev_conv_dw_w, ev_conv_dw_b,
           ev_conv_ln_g, ev_conv_ln_b, ev_ret_ln_g, ev_w_out, od_w_in, od_gm_ln_g, od_gm_ln_b,
           od_gm_ws, od_gm_bs, od_w_out, ffn_w_up, ffn_dw_w, ffn_dw_b, ffn_w_down):
    bsz, s, d = x.shape
    xf = x.reshape(bsz * s, d)
    h = rms_norm_bf16(xf, mix_norm_g[0])
    for i in range(DEPTH):
        j = i // 2
        if i % 2 == 0:
            p = matmul(h, ev_w_in[j].astype(BF16), name="even_in")
            b_out = retention(p, ev_ret_ln_g[j])
            xf, h = conformer_out(p, ev_conv_dw_w[j], ev_conv_dw_b[j], ev_conv_ln_g[j],
                                  ev_conv_ln_b[j], b_out, ev_w_out[j].astype(BF16), xf,
                                  ffn_norm_g[i])
        else:
            p = matmul(h, od_w_in[j].astype(BF16), act="gelu", name="odd_in")
            xf, h = spatial_gate_out(p, od_gm_ln_g[j], od_gm_ln_b[j], od_gm_ws[j], od_gm_bs[j],
                                     od_w_out[j].astype(BF16), xf, ffn_norm_g[i])
        gated = ffn_up_gate(h, ffn_w_up[i].astype(BF16), ffn_dw_w[i], ffn_dw_b[i])
        w_down = ffn_w_down[i].astype(BF16)
        if i < DEPTH - 1:
            xf, h = matmul_residual_norm(gated, w_down, xf, mix_norm_g[i + 1], name="ffn_down")
        else:
            h = matmul_residual_norm(gated, w_down, xf, final_norm_g, h_dtype=F32, emit_x=False,
                                     name="ffn_down")
    return h.reshape(bsz, s, d)
```

```python
import functools

import jax
import jax.numpy as jnp
import numpy as np
from jax import lax
from jax.experimental import pallas as pl
from jax.experimental.pallas import tpu as pltpu

D_MODEL = 2048
SEQ = 2048
DEPTH = 4
CHUNK = 64
CONV_CH = 1024
CONV_K = 31
RET_HEADS = 4
RET_DK = 256
RET_DV = 256
GM_WIDTH = 2048
GM_BLOCK = 128
GM_GROUPS = 4
D_FF = 5632
FFN_K = 3
ROPE_THETA = 10000.0
RMS_EPS = 1e-6
LN_EPS = 1e-5

RET_BLOCK = 256
RET_HEADS_PER_STEP = 2
CONV_HALO = 32
LANES = 128

VMEM_LIMIT = 56 * 1024 * 1024

BF16 = jnp.bfloat16
F32 = jnp.float32


def _params(sem):
    return pltpu.CompilerParams(dimension_semantics=sem, vmem_limit_bytes=VMEM_LIMIT)


def _norm_mm_kernel(x_ref, g_ref, w_ref, o_ref, h_ref):
    @pl.when(pl.program_id(1) == 0)
    def _():
        x = x_ref[...]
        ms = jnp.mean(x * x, axis=-1, keepdims=True)
        h_ref[...] = (x * lax.rsqrt(ms + RMS_EPS) * g_ref[...]).astype(h_ref.dtype)

    o_ref[...] = jnp.dot(h_ref[...], w_ref[...], preferred_element_type=F32).astype(o_ref.dtype)


def norm_matmul(x, g, w, *, out_dtype=F32, tm=1024, tn=1024, name="norm_proj"):
    m, k = x.shape
    n = w.shape[1]
    return pl.pallas_call(
        _norm_mm_kernel,
        grid=(m // tm, n // tn),
        in_specs=[pl.BlockSpec((tm, k), lambda i, j: (i, 0)),
                  pl.BlockSpec((1, k), lambda i, j: (0, 0)),
                  pl.BlockSpec((k, tn), lambda i, j: (0, j))],
        out_specs=pl.BlockSpec((tm, tn), lambda i, j: (i, j)),
        out_shape=jax.ShapeDtypeStruct((m, n), out_dtype),
        scratch_shapes=[pltpu.VMEM((tm, k), BF16)],
        compiler_params=_params(("parallel", "arbitrary")),
        name=name,
    )(x, g.reshape(1, k), w)


def _mm_kernel(a_ref, w_ref, o_ref, *, act):
    acc = jnp.dot(a_ref[...], w_ref[...], preferred_element_type=F32)
    if act == "gelu":
        acc = 0.5 * acc * (1.0 + lax.erf(acc * np.float32(np.sqrt(0.5))))
    o_ref[...] = acc.astype(o_ref.dtype)


def matmul(a, w, *, act=None, out_dtype=F32, tm=1024, tn=1024, name="proj"):
    m, k = a.shape
    n = w.shape[1]
    return pl.pallas_call(
        functools.partial(_mm_kernel, act=act),
        grid=(m // tm, n // tn),
        in_specs=[pl.BlockSpec((tm, k), lambda i, j: (i, 0)),
                  pl.BlockSpec((k, tn), lambda i, j: (0, j))],
        out_specs=pl.BlockSpec((tm, tn), lambda i, j: (i, j)),
        out_shape=jax.ShapeDtypeStruct((m, n), out_dtype),
        compiler_params=_params(("parallel", "arbitrary")),
        name=name,
    )(a, w)


def _rms_finish(xn, g_ref, xo_ref, ho_ref, rows):
    if xo_ref is not None:
        xo_ref[rows, :] = xn
    ms = jnp.mean(xn * xn, axis=-1, keepdims=True)
    ho_ref[rows, :] = (xn * lax.rsqrt(ms + RMS_EPS) * g_ref[...]).astype(ho_ref.dtype)


def _mm_res_kernel(a_ref, w_ref, x_ref, g_ref, *out_refs, tm, rc):
    ho_ref = out_refs[-1]
    xo_ref = out_refs[0] if len(out_refs) == 2 else None
    for r in range(tm // rc):
        rows = slice(r * rc, (r + 1) * rc)
        xn = x_ref[rows, :] + jnp.dot(a_ref[rows, :], w_ref[...], preferred_element_type=F32)
        _rms_finish(xn, g_ref, xo_ref, ho_ref, rows)


def matmul_residual_norm(a, w, x, g, *, h_dtype=BF16, emit_x=True, tm=512, rc=256,
                         name="proj_res"):
    m, ka = a.shape
    d = w.shape[1]
    row = lambda i: (i, 0)
    fixed = lambda i: (0, 0)
    h_shape = jax.ShapeDtypeStruct((m, d), h_dtype)
    if emit_x:
        out_specs = [pl.BlockSpec((tm, d), row), pl.BlockSpec((tm, d), row)]
        out_shape = [jax.ShapeDtypeStruct((m, d), F32), h_shape]
    else:
        out_specs = pl.BlockSpec((tm, d), row)
        out_shape = h_shape
    return pl.pallas_call(
        functools.partial(_mm_res_kernel, tm=tm, rc=rc),
        grid=(m // tm,),
        in_specs=[pl.BlockSpec((tm, ka), row),
                  pl.BlockSpec((ka, d), fixed, pipeline_mode=pl.Buffered(1)),
                  pl.BlockSpec((tm, d), row),
                  pl.BlockSpec((1, d), fixed)],
        out_specs=out_specs,
        out_shape=out_shape,
        compiler_params=_params(("parallel",)),
        name=name,
    )(a, w, x, g.reshape(1, d))


FFN_PAD = 8
FFN_ROW_CHUNKS = (1024, 1024)


def _ffn_up_kernel(h_ref, wa_ref, wb_ref, cwa_ref, cwb_ref, cba_ref, cbb_ref, o_ref, s_ref,
                   *, chunks, tn):
    nt = tn // LANES
    w_refs = (wa_ref, wb_ref)
    cw = (cwa_ref[...], cwb_ref[...])
    cb = (cba_ref[...], cbb_ref[...])
    for ab in range(2):
        for c in range(nt):
            s_ref[ab, c, 0:FFN_PAD, :] = jnp.zeros((FFN_PAD, LANES), F32)
    r0 = 0
    for rc in chunks:
        hr = h_ref[r0:r0 + rc, :]
        u = [jnp.dot(hr, w_refs[ab][...], preferred_element_type=F32) for ab in range(2)]
        for ab in range(2):
            for c in range(nt):
                s_ref[ab, c, FFN_PAD + r0:FFN_PAD + r0 + rc, :] = u[ab][:, c * LANES:(c + 1) * LANES]
        for c in range(nt):
            lanes = slice(c * LANES, (c + 1) * LANES)
            y = []
            for ab in range(2):
                u1 = s_ref[ab, c, FFN_PAD + r0 - 1:FFN_PAD + r0 - 1 + rc, :]
                u2 = s_ref[ab, c, FFN_PAD + r0 - 2:FFN_PAD + r0 - 2 + rc, :]
                y.append(cw[ab][2:3, lanes] * u[ab][:, lanes] + cw[ab][1:2, lanes] * u1
                         + cw[ab][0:1, lanes] * u2 + cb[ab][:, lanes])
            a, b = y
            o_ref[r0:r0 + rc, lanes] = (a * jax.nn.sigmoid(a) * b).astype(o_ref.dtype)
        r0 += rc


def ffn_up_gate(h, w_up, dw_w, dw_b, *, tn=512):
    m, k = h.shape
    nj = D_FF // tn
    dw_b = dw_b.reshape(1, 2 * D_FF)
    assert sum(FFN_ROW_CHUNKS) == SEQ
    return pl.pallas_call(
        functools.partial(_ffn_up_kernel, chunks=FFN_ROW_CHUNKS, tn=tn),
        grid=(m // SEQ, nj),
        in_specs=[pl.BlockSpec((SEQ, k), lambda i, j: (i, 0)),
                  pl.BlockSpec((k, tn), lambda i, j: (0, j)),
                  pl.BlockSpec((k, tn), lambda i, j: (0, j + nj)),
                  pl.BlockSpec((FFN_K, tn), lambda i, j: (0, j)),
                  pl.BlockSpec((FFN_K, tn), lambda i, j: (0, j + nj)),
                  pl.BlockSpec((1, tn), lambda i, j: (0, j)),
                  pl.BlockSpec((1, tn), lambda i, j: (0, j + nj))],
        out_specs=pl.BlockSpec((SEQ, tn), lambda i, j: (i, j)),
        out_shape=jax.ShapeDtypeStruct((m, D_FF), BF16),
        scratch_shapes=[pltpu.VMEM((2, tn // LANES, FFN_PAD + SEQ, LANES), F32)],
        compiler_params=_params(("parallel", "arbitrary")),
        name="ffn_up_gate",
    )(h, w_up, w_up, dw_w, dw_w, dw_b, dw_b)


def _conformer_out_kernel(ga_ref, gb_ref, ha_ref, hb_ref, w_ref, b_ref, lg_ref, lb_ref, r_ref,
                          wa_ref, wb_ref, x_ref, g_ref, xo_ref, ho_ref, u_ref, y_ref, a_ref,
                          *, ts, rc, sub, tiles_per_seq):
    i = pl.program_id(0)
    nc = CONV_CH // LANES
    not_first = (i % tiles_per_seq != 0).astype(F32)
    uh = ha_ref[...] * jax.nn.sigmoid(hb_ref[...]) * not_first
    u = ga_ref[...] * jax.nn.sigmoid(gb_ref[...])
    for c in range(nc):
        lanes = slice(c * LANES, (c + 1) * LANES)
        u_ref[c, 0:CONV_HALO, :] = uh[:, lanes]
        u_ref[c, CONV_HALO:CONV_HALO + ts, :] = u[:, lanes]
    base = CONV_HALO - (CONV_K - 1)
    for r in range(ts // rc):
        for c in range(nc):
            lanes = slice(c * LANES, (c + 1) * LANES)
            for q in range(rc // sub):
                r0 = r * rc + q * sub
                acc = jnp.broadcast_to(b_ref[:, lanes], (sub, LANES))
                for k in range(CONV_K):
                    acc = acc + w_ref[k:k + 1, lanes] * u_ref[c, r0 + base + k:r0 + base + k + sub, :]
                y_ref[r0:r0 + sub, lanes] = acc
        rows = slice(r * rc, (r + 1) * rc)
        y = y_ref[rows, :]
        mu = jnp.mean(y, axis=-1, keepdims=True)
        yc = y - mu
        var = jnp.mean(yc * yc, axis=-1, keepdims=True)
        yn = yc * lax.rsqrt(var + LN_EPS) * lg_ref[...] + lb_ref[...]
        a_ref[rows, :] = (yn * jax.nn.sigmoid(yn)).astype(BF16)
        xn = (x_ref[rows, :]
              + jnp.dot(a_ref[rows, :], wa_ref[...], preferred_element_type=F32)
              + jnp.dot(r_ref[rows, :], wb_ref[...], preferred_element_type=F32))
        _rms_finish(xn, g_ref, xo_ref, ho_ref, rows)


def conformer_out(p, w_dw, b_dw, ln_g, ln_b, ret_out, w_out, x, g, *, ts=512, rc=256, sub=128):
    m = p.shape[0]
    d = w_out.shape[1]
    tiles_per_seq = SEQ // ts
    halo_per_tile = ts // CONV_HALO
    row = lambda i: (i, 0)
    fixed2 = lambda i: (0, 0)

    def halo_idx(col):
        return lambda i: (jnp.maximum(i * halo_per_tile - 1, 0), col)

    return pl.pallas_call(
        functools.partial(_conformer_out_kernel, ts=ts, rc=rc, sub=sub,
                          tiles_per_seq=tiles_per_seq),
        grid=(m // ts,),
        in_specs=[pl.BlockSpec((ts, CONV_CH), row),
                  pl.BlockSpec((ts, CONV_CH), lambda i: (i, 1)),
                  pl.BlockSpec((CONV_HALO, CONV_CH), halo_idx(0)),
                  pl.BlockSpec((CONV_HALO, CONV_CH), halo_idx(1)),
                  pl.BlockSpec((CONV_K, CONV_CH), fixed2),
                  pl.BlockSpec((1, CONV_CH), fixed2),
                  pl.BlockSpec((1, CONV_CH), fixed2),
                  pl.BlockSpec((1, CONV_CH), fixed2),
                  pl.BlockSpec((ts, CONV_CH), row),
                  pl.BlockSpec((CONV_CH, d), fixed2, pipeline_mode=pl.Buffered(1)),
                  pl.BlockSpec((CONV_CH, d), lambda i: (1, 0), pipeline_mode=pl.Buffered(1)),
                  pl.BlockSpec((ts, d), row),
                  pl.BlockSpec((1, d), fixed2)],
        out_specs=[pl.BlockSpec((ts, d), row), pl.BlockSpec((ts, d), row)],
        out_shape=[jax.ShapeDtypeStruct((m, d), F32), jax.ShapeDtypeStruct((m, d), BF16)],
        scratch_shapes=[pltpu.VMEM((CONV_CH // LANES, CONV_HALO + ts, LANES), F32),
                        pltpu.VMEM((ts, CONV_CH), F32),
                        pltpu.VMEM((ts, CONV_CH), BF16)],
        compiler_params=_params(("parallel",)),
        name="conformer_out",
    )(p, p, p, p, w_dw, b_dw.reshape(1, CONV_CH), ln_g.reshape(1, CONV_CH),
      ln_b.reshape(1, CONV_CH), ret_out, w_out, w_out, x, g.reshape(1, d))


def _retention_tables():
    t = RET_BLOCK
    log_gamma = jnp.log1p(-jnp.exp2(-5.0 - jnp.arange(RET_HEADS, dtype=F32)))
    pos = jnp.arange(t, dtype=F32)
    dist = jnp.abs(pos[:, None] - pos[None, :])
    chunk = jnp.arange(t) // CHUNK
    mask = chunk[None, :] <= chunk[:, None]
    d = jnp.where(mask[None], jnp.exp(dist[None] * log_gamma[:, None, None]), 0.0)
    xi = jnp.exp((pos + 1.0)[None] * log_gamma[:, None])
    zeta = jnp.exp((t - 1.0 - pos)[None] * log_gamma[:, None])
    gt = jnp.exp(t * log_gamma)
    xi = jnp.broadcast_to(xi[:, :, None], (RET_HEADS, t, RET_DV))
    zeta = jnp.broadcast_to(zeta[:, :, None], (RET_HEADS, t, RET_DK))
    gt = jnp.broadcast_to(gt[:, None, None], (RET_HEADS, 1, RET_DV))
    inv = ROPE_THETA ** (-jnp.arange(0, RET_DK, 2, dtype=F32) / RET_DK)
    ang = jnp.arange(SEQ, dtype=F32)[:, None] * inv[None, :]
    return d, xi, zeta, gt, jnp.cos(ang), jnp.sin(ang)


def _rotary(x, cos, sin):
    half = x.shape[-1] // 2
    x1, x2 = x[:, :half], x[:, half:]
    return jnp.concatenate([x1 * cos - x2 * sin, x1 * sin + x2 * cos], axis=-1)


def _retention_kernel(q_ref, k_ref, v_ref, gate_ref, cos_ref, sin_ref, d_ref, xi_ref, zeta_ref,
                      gt_ref, lg_ref, o_ref, state_ref):
    t = RET_BLOCK
    state_ref[...] = jnp.zeros_like(state_ref)
    scale = np.float32(RET_DK ** -0.5)

    def body(blk, carry):
        r0 = pl.multiple_of(blk * t, t)
        rows = pl.ds(r0, t)
        cos = cos_ref[rows, :]
        sin = sin_ref[rows, :]
        for hd in range(RET_HEADS_PER_STEP):
            kc = slice(hd * RET_DK, (hd + 1) * RET_DK)
            vc = slice(hd * RET_DV, (hd + 1) * RET_DV)
            q = _rotary(q_ref[rows, kc], cos, sin)
            k = _rotary(k_ref[rows, kc], cos, sin) * scale
            v = v_ref[rows, vc].astype(BF16)
            qb = q.astype(BF16)
            s = lax.dot_general(qb, k.astype(BF16), (((1,), (1,)), ((), ())),
                                preferred_element_type=F32) * d_ref[hd]
            state = state_ref[hd]
            o = jnp.dot(s.astype(BF16), v, preferred_element_type=F32)
            o = o + xi_ref[hd] * jnp.dot(qb, state.astype(BF16), preferred_element_type=F32)
            kz = (k * zeta_ref[hd]).astype(BF16)
            state_ref[hd] = gt_ref[hd] * state + lax.dot_general(
                kz, v, (((0,), (0,)), ((), ())), preferred_element_type=F32)
            mu = jnp.mean(o, axis=-1, keepdims=True)
            oc = o - mu
            var = jnp.mean(oc * oc, axis=-1, keepdims=True)
            on = oc * lax.rsqrt(var + LN_EPS) * lg_ref[:, vc]
            g = gate_ref[rows, vc]
            o_ref[rows, vc] = (on * (g * jax.nn.sigmoid(g))).astype(o_ref.dtype)
        return carry

    lax.fori_loop(0, SEQ // t, body, 0)


def retention(p, ln_g):
    m = p.shape[0]
    d, xi, zeta, gt, cos, sin = _retention_tables()
    t = RET_BLOCK
    hp = RET_HEADS_PER_STEP
    groups = RET_HEADS // hp
    col0 = 2 * CONV_CH // (hp * RET_DK)

    def col(off):
        return lambda b, h: (b, col0 + off * groups + h)

    return pl.pallas_call(
        _retention_kernel,
        grid=(m // SEQ, groups),
        in_specs=[pl.BlockSpec((SEQ, hp * RET_DK), col(0)),
                  pl.BlockSpec((SEQ, hp * RET_DK), col(1)),
                  pl.BlockSpec((SEQ, hp * RET_DV), col(2)),
                  pl.BlockSpec((SEQ, hp * RET_DV), col(3)),
                  pl.BlockSpec((SEQ, RET_DK // 2), lambda b, h: (0, 0)),
                  pl.BlockSpec((SEQ, RET_DK // 2), lambda b, h: (0, 0)),
                  pl.BlockSpec((hp, t, t), lambda b, h: (h, 0, 0)),
                  pl.BlockSpec((hp, t, RET_DV), lambda b, h: (h, 0, 0)),
                  pl.BlockSpec((hp, t, RET_DK), lambda b, h: (h, 0, 0)),
                  pl.BlockSpec((hp, 1, RET_DV), lambda b, h: (h, 0, 0)),
                  pl.BlockSpec((1, hp * RET_DV), lambda b, h: (0, h))],
        out_specs=pl.BlockSpec((SEQ, hp * RET_DV), lambda b, h: (b, h)),
        out_shape=jax.ShapeDtypeStruct((m, RET_HEADS * RET_DV), BF16),
        scratch_shapes=[pltpu.VMEM((hp, RET_DK, RET_DV), F32)],
        compiler_params=_params(("parallel", "arbitrary")),
        name="retention",
    )(p, p, p, p, cos, sin, d, xi, zeta, gt, ln_g.reshape(1, RET_HEADS * RET_DV))


def _sgu_out_kernel(u_ref, v_ref, lg_ref, lb_ref, ws_ref, bs_ref, w_ref, x_ref, g_ref,
                    xo_ref, ho_ref, vn_ref, z_ref, *, tm, rc):
    ri = lax.broadcasted_iota(jnp.int32, (GM_BLOCK, GM_BLOCK), 0) // CHUNK
    ci = lax.broadcasted_iota(jnp.int32, (GM_BLOCK, GM_BLOCK), 1) // CHUNK
    gw = GM_WIDTH // GM_GROUPS
    wm = [jnp.where(ci <= ri, ws_ref[g], 0.0).astype(BF16) for g in range(GM_GROUPS)]
    for r in range(tm // rc):
        for n in range(r * rc // GM_BLOCK, (r + 1) * rc // GM_BLOCK):
            blk = slice(n * GM_BLOCK, (n + 1) * GM_BLOCK)
            v = v_ref[blk, :]
            mu = jnp.mean(v, axis=-1, keepdims=True)
            vc = v - mu
            var = jnp.mean(vc * vc, axis=-1, keepdims=True)
            vn_ref[blk, :] = (vc * lax.rsqrt(var + LN_EPS) * lg_ref[...] + lb_ref[...]).astype(BF16)
            for g in range(GM_GROUPS):
                cols = slice(g * gw, (g + 1) * gw)
                mixed = jnp.dot(wm[g], vn_ref[blk, cols], preferred_element_type=F32) + bs_ref[g]
                z_ref[blk, cols] = (u_ref[blk, cols] * mixed).astype(BF16)
        rows = slice(r * rc, (r + 1) * rc)
        xn = x_ref[rows, :] + jnp.dot(z_ref[rows, :], w_ref[...], preferred_element_type=F32)
        _rms_finish(xn, g_ref, xo_ref, ho_ref, rows)


def spatial_gate_out(p, ln_g, ln_b, ws, bs, w_out, x, g, *, tm=512, rc=256):
    m = p.shape[0]
    d = w_out.shape[1]
    row = lambda i: (i, 0)
    fixed2 = lambda i: (0, 0)
    return pl.pallas_call(
        functools.partial(_sgu_out_kernel, tm=tm, rc=rc),
        grid=(m // tm,),
        in_specs=[pl.BlockSpec((tm, GM_WIDTH), row),
                  pl.BlockSpec((tm, GM_WIDTH), lambda i: (i, 1)),
                  pl.BlockSpec((1, GM_WIDTH), fixed2),
                  pl.BlockSpec((1, GM_WIDTH), fixed2),
                  pl.BlockSpec((GM_GROUPS, GM_BLOCK, GM_BLOCK), lambda i: (0, 0, 0)),
                  pl.BlockSpec((GM_GROUPS, GM_BLOCK, 1), lambda i: (0, 0, 0)),
                  pl.BlockSpec((GM_WIDTH, d), fixed2, pipeline_mode=pl.Buffered(1)),
                  pl.BlockSpec((tm, d), row),
                  pl.BlockSpec((1, d), fixed2)],
        out_specs=[pl.BlockSpec((tm, d), row), pl.BlockSpec((tm, d), row)],
        out_shape=[jax.ShapeDtypeStruct((m, d), F32), jax.ShapeDtypeStruct((m, d), BF16)],
        scratch_shapes=[pltpu.VMEM((tm, GM_WIDTH), BF16), pltpu.VMEM((tm, GM_WIDTH), BF16)],
        compiler_params=_params(("parallel",)),
        name="spatial_gate_out",
    )(p, p, ln_g.reshape(1, GM_WIDTH), ln_b.reshape(1, GM_WIDTH), ws,
      bs.reshape(GM_GROUPS, GM_BLOCK, 1), w_out, x, g.reshape(1, d))


def kernel(x, mix_norm_g, ffn_norm_g, final_norm_g, ev_w_in, ev_conv_dw_w, ev_conv_dw_b,
           ev_conv_ln_g, ev_conv_ln_b, ev_ret_ln_g, ev_w_out, od_w_in, od_gm_ln_g, od_gm_ln_b,
           od_gm_ws, od_gm_bs, od_w_out, ffn_w_up, ffn_dw_w, ffn_dw_b, ffn_w_down):
    bsz, s, d = x.shape
    xf = x.reshape(bsz * s, d)
    h = None
    for i in range(DEPTH):
        j = i // 2
        if i % 2 == 0:
            w_in = ev_w_in[j].astype(BF16)
            if h is None:
                p = norm_matmul(xf, mix_norm_g[i], w_in, name="even_in")
            else:
                p = matmul(h, w_in, tm=SEQ, name="even_in")
            b_out = retention(p, ev_ret_ln_g[j])
            xf, h = conformer_out(p, ev_conv_dw_w[j], ev_conv_dw_b[j], ev_conv_ln_g[j],
                                  ev_conv_ln_b[j], b_out, ev_w_out[j].astype(BF16), xf,
                                  ffn_norm_g[i])
        else:
            p = matmul(h, od_w_in[j].astype(BF16), act="gelu", name="odd_in")
            xf, h = spatial_gate_out(p, od_gm_ln_g[j], od_gm_ln_b[j], od_gm_ws[j], od_gm_bs[j],
                                     od_w_out[j].astype(BF16), xf, ffn_norm_g[i])
        gated = ffn_up_gate(h, ffn_w_up[i].astype(BF16), ffn_dw_w[i], ffn_dw_b[i])
        w_down = ffn_w_down[i].astype(BF16)
        if i < DEPTH - 1:
            xf, h = matmul_residual_norm(gated, w_down, xf, mix_norm_g[i + 1], name="ffn_down")
        else:
            h = matmul_residual_norm(gated, w_down, xf, final_norm_g, h_dtype=F32, emit_x=False,
                                     name="ffn_down")
    return h.reshape(bsz, s, d)
```

```python
import functools

import jax
import jax.numpy as jnp
import numpy as np
from jax import lax
from jax.experimental import pallas as pl
from jax.experimental.pallas import tpu as pltpu

D_MODEL = 2048
SEQ = 2048
DEPTH = 4
CHUNK = 64
CONV_CH = 1024
CONV_K = 31
RET_HEADS = 4
RET_DK = 256
RET_DV = 256
GM_WIDTH = 2048
GM_BLOCK = 128
GM_GROUPS = 4
D_FF = 5632
FFN_K = 3
ROPE_THETA = 10000.0
RMS_EPS = 1e-6
LN_EPS = 1e-5

RET_BLOCK = 256
RET_HEADS_PER_STEP = 2
CONV_HALO = 32
LANES = 128

VMEM_LIMIT = 56 * 1024 * 1024

BF16 = jnp.bfloat16
F32 = jnp.float32


def _params(sem):
    return pltpu.CompilerParams(dimension_semantics=sem, vmem_limit_bytes=VMEM_LIMIT)


def _norm_mm_kernel(x_ref, g_ref, w_ref, o_ref, h_ref):
    @pl.when(pl.program_id(1) == 0)
    def _():
        x = x_ref[...]
        ms = jnp.mean(x * x, axis=-1, keepdims=True)
        h_ref[...] = (x * lax.rsqrt(ms + RMS_EPS) * g_ref[...]).astype(h_ref.dtype)

    o_ref[...] = jnp.dot(h_ref[...], w_ref[...], preferred_element_type=F32).astype(o_ref.dtype)


def norm_matmul(x, g, w, layer, *, out_dtype=F32, tm=1024, tn=1024, name="norm_proj"):
    m, k = x.shape
    n = w.shape[2]
    return pl.pallas_call(
        _norm_mm_kernel,
        grid=(m // tm, n // tn),
        in_specs=[pl.BlockSpec((tm, k), lambda i, j: (i, 0)),
                  pl.BlockSpec((1, k), lambda i, j: (0, 0)),
                  pl.BlockSpec((None, k, tn), lambda i, j: (layer, 0, j))],
        out_specs=pl.BlockSpec((tm, tn), lambda i, j: (i, j)),
        out_shape=jax.ShapeDtypeStruct((m, n), out_dtype),
        scratch_shapes=[pltpu.VMEM((tm, k), BF16)],
        compiler_params=_params(("parallel", "arbitrary")),
        name=name,
    )(x, g.reshape(1, k), w)


def _mm_kernel(a_ref, w_ref, o_ref, *, act):
    acc = jnp.dot(a_ref[...], w_ref[...], preferred_element_type=F32)
    if act == "gelu":
        acc = 0.5 * acc * (1.0 + lax.erf(acc * np.float32(np.sqrt(0.5))))
    o_ref[...] = acc.astype(o_ref.dtype)


def matmul(a, w, layer, *, act=None, out_dtype=F32, tm=1024, tn=1024, name="proj"):
    m, k = a.shape
    n = w.shape[2]
    return pl.pallas_call(
        functools.partial(_mm_kernel, act=act),
        grid=(m // tm, n // tn),
        in_specs=[pl.BlockSpec((tm, k), lambda i, j: (i, 0)),
                  pl.BlockSpec((None, k, tn), lambda i, j: (layer, 0, j))],
        out_specs=pl.BlockSpec((tm, tn), lambda i, j: (i, j)),
        out_shape=jax.ShapeDtypeStruct((m, n), out_dtype),
        compiler_params=_params(("parallel", "arbitrary")),
        name=name,
    )(a, w)


def _rms_finish(xn, g_ref, xo_ref, ho_ref, rows):
    if xo_ref is not None:
        xo_ref[rows, :] = xn
    ms = jnp.mean(xn * xn, axis=-1, keepdims=True)
    ho_ref[rows, :] = (xn * lax.rsqrt(ms + RMS_EPS) * g_ref[...]).astype(ho_ref.dtype)


def _mm_res_kernel(a_ref, w_ref, x_ref, g_ref, *out_refs, tm, rc):
    ho_ref = out_refs[-1]
    xo_ref = out_refs[0] if len(out_refs) == 2 else None
    for r in range(tm // rc):
        rows = slice(r * rc, (r + 1) * rc)
        xn = x_ref[rows, :] + jnp.dot(a_ref[rows, :], w_ref[...], preferred_element_type=F32)
        _rms_finish(xn, g_ref, xo_ref, ho_ref, rows)


def matmul_residual_norm(a, w, layer, x, g, *, h_dtype=BF16, emit_x=True, tm=512, rc=256,
                         name="proj_res"):
    m, ka = a.shape
    d = w.shape[2]
    row = lambda i: (i, 0)
    fixed = lambda i: (0, 0)
    h_shape = jax.ShapeDtypeStruct((m, d), h_dtype)
    if emit_x:
        out_specs = [pl.BlockSpec((tm, d), row), pl.BlockSpec((tm, d), row)]
        out_shape = [jax.ShapeDtypeStruct((m, d), F32), h_shape]
    else:
        out_specs = pl.BlockSpec((tm, d), row)
        out_shape = h_shape
    return pl.pallas_call(
        functools.partial(_mm_res_kernel, tm=tm, rc=rc),
        grid=(m // tm,),
        in_specs=[pl.BlockSpec((tm, ka), row),
                  pl.BlockSpec((None, ka, d), lambda i: (layer, 0, 0),
                               pipeline_mode=pl.Buffered(1)),
                  pl.BlockSpec((tm, d), row),
                  pl.BlockSpec((1, d), fixed)],
        out_specs=out_specs,
        out_shape=out_shape,
        compiler_params=_params(("parallel",)),
        name=name,
    )(a, w, x, g.reshape(1, d))


FFN_PAD = 8
FFN_ROW_CHUNKS = (1024, 1024)


def _ffn_up_kernel(h_ref, wa_ref, wb_ref, cwa_ref, cwb_ref, cba_ref, cbb_ref, o_ref, s_ref,
                   *, chunks, tn):
    nt = tn // LANES
    w_refs = (wa_ref, wb_ref)
    cw = (cwa_ref[...], cwb_ref[...])
    cb = (cba_ref[...], cbb_ref[...])
    for ab in range(2):
        for c in range(nt):
            s_ref[ab, c, 0:FFN_PAD, :] = jnp.zeros((FFN_PAD, LANES), F32)
    r0 = 0
    for rc in chunks:
        hr = h_ref[r0:r0 + rc, :]
        u = [jnp.dot(hr, w_refs[ab][...], preferred_element_type=F32) for ab in range(2)]
        for ab in range(2):
            for c in range(nt):
                s_ref[ab, c, FFN_PAD + r0:FFN_PAD + r0 + rc, :] = u[ab][:, c * LANES:(c + 1) * LANES]
        for c in range(nt):
            lanes = slice(c * LANES, (c + 1) * LANES)
            y = []
            for ab in range(2):
                u1 = s_ref[ab, c, FFN_PAD + r0 - 1:FFN_PAD + r0 - 1 + rc, :]
                u2 = s_ref[ab, c, FFN_PAD + r0 - 2:FFN_PAD + r0 - 2 + rc, :]
                y.append(cw[ab][2:3, lanes] * u[ab][:, lanes] + cw[ab][1:2, lanes] * u1
                         + cw[ab][0:1, lanes] * u2 + cb[ab][:, lanes])
            a, b = y
            o_ref[r0:r0 + rc, lanes] = (a * jax.nn.sigmoid(a) * b).astype(o_ref.dtype)
        r0 += rc


def ffn_up_gate(h, w_up, layer, dw_w, dw_b, *, tn=512):
    m, k = h.shape
    nj = D_FF // tn
    dw_b = dw_b.reshape(1, 2 * D_FF)
    assert sum(FFN_ROW_CHUNKS) == SEQ
    return pl.pallas_call(
        functools.partial(_ffn_up_kernel, chunks=FFN_ROW_CHUNKS, tn=tn),
        grid=(m // SEQ, nj),
        in_specs=[pl.BlockSpec((SEQ, k), lambda i, j: (i, 0)),
                  pl.BlockSpec((None, k, tn), lambda i, j: (layer, 0, j)),
                  pl.BlockSpec((None, k, tn), lambda i, j: (layer, 0, j + nj)),
                  pl.BlockSpec((FFN_K, tn), lambda i, j: (0, j)),
                  pl.BlockSpec((FFN_K, tn), lambda i, j: (0, j + nj)),
                  pl.BlockSpec((1, tn), lambda i, j: (0, j)),
                  pl.BlockSpec((1, tn), lambda i, j: (0, j + nj))],
        out_specs=pl.BlockSpec((SEQ, tn), lambda i, j: (i, j)),
        out_shape=jax.ShapeDtypeStruct((m, D_FF), BF16),
        scratch_shapes=[pltpu.VMEM((2, tn // LANES, FFN_PAD + SEQ, LANES), F32)],
        compiler_params=_params(("parallel", "arbitrary")),
        name="ffn_up_gate",
    )(h, w_up, w_up, dw_w, dw_w, dw_b, dw_b)


def _conformer_out_kernel(ga_ref, gb_ref, ha_ref, hb_ref, w_ref, b_ref, lg_ref, lb_ref, r_ref,
                          wa_ref, wb_ref, x_ref, g_ref, xo_ref, ho_ref, u_ref, y_ref, a_ref,
                          *, ts, rc, sub, tiles_per_seq):
    i = pl.program_id(0)
    nc = CONV_CH // LANES
    not_first = (i % tiles_per_seq != 0).astype(F32)
    uh = ha_ref[...] * jax.nn.sigmoid(hb_ref[...]) * not_first
    u = ga_ref[...] * jax.nn.sigmoid(gb_ref[...])
    for c in range(nc):
        lanes = slice(c * LANES, (c + 1) * LANES)
        u_ref[c, 0:CONV_HALO, :] = uh[:, lanes]
        u_ref[c, CONV_HALO:CONV_HALO + ts, :] = u[:, lanes]
    base = CONV_HALO - (CONV_K - 1)
    for r in range(ts // rc):
        for c in range(nc):
            lanes = slice(c * LANES, (c + 1) * LANES)
            for q in range(rc // sub):
                r0 = r * rc + q * sub
                acc = jnp.broadcast_to(b_ref[:, lanes], (sub, LANES))
                for k in range(CONV_K):
                    acc = acc + w_ref[k:k + 1, lanes] * u_ref[c, r0 + base + k:r0 + base + k + sub, :]
                y_ref[r0:r0 + sub, lanes] = acc
        rows = slice(r * rc, (r + 1) * rc)
        y = y_ref[rows, :]
        mu = jnp.mean(y, axis=-1, keepdims=True)
        yc = y - mu
        var = jnp.mean(yc * yc, axis=-1, keepdims=True)
        yn = yc * lax.rsqrt(var + LN_EPS) * lg_ref[...] + lb_ref[...]
        a_ref[rows, :] = (yn * jax.nn.sigmoid(yn)).astype(BF16)
        xn = (x_ref[rows, :]
              + jnp.dot(a_ref[rows, :], wa_ref[...], preferred_element_type=F32)
              + jnp.dot(r_ref[rows, :], wb_ref[...], preferred_element_type=F32))
        _rms_finish(xn, g_ref, xo_ref, ho_ref, rows)


def conformer_out(p, w_dw, b_dw, ln_g, ln_b, ret_out, w_out, layer, x, g, *, ts=512, rc=256,
                  sub=128):
    m = p.shape[0]
    d = w_out.shape[2]
    tiles_per_seq = SEQ // ts
    halo_per_tile = ts // CONV_HALO
    row = lambda i: (i, 0)
    fixed2 = lambda i: (0, 0)

    def halo_idx(col):
        return lambda i: (jnp.maximum(i * halo_per_tile - 1, 0), col)

    return pl.pallas_call(
        functools.partial(_conformer_out_kernel, ts=ts, rc=rc, sub=sub,
                          tiles_per_seq=tiles_per_seq),
        grid=(m // ts,),
        in_specs=[pl.BlockSpec((ts, CONV_CH), row),
                  pl.BlockSpec((ts, CONV_CH), lambda i: (i, 1)),
                  pl.BlockSpec((CONV_HALO, CONV_CH), halo_idx(0)),
                  pl.BlockSpec((CONV_HALO, CONV_CH), halo_idx(1)),
                  pl.BlockSpec((CONV_K, CONV_CH), fixed2),
                  pl.BlockSpec((1, CONV_CH), fixed2),
                  pl.BlockSpec((1, CONV_CH), fixed2),
                  pl.BlockSpec((1, CONV_CH), fixed2),
                  pl.BlockSpec((ts, CONV_CH), row),
                  pl.BlockSpec((None, CONV_CH, d), lambda i: (layer, 0, 0),
                               pipeline_mode=pl.Buffered(1)),
                  pl.BlockSpec((None, CONV_CH, d), lambda i: (layer, 1, 0),
                               pipeline_mode=pl.Buffered(1)),
                  pl.BlockSpec((ts, d), row),
                  pl.BlockSpec((1, d), fixed2)],
        out_specs=[pl.BlockSpec((ts, d), row), pl.BlockSpec((ts, d), row)],
        out_shape=[jax.ShapeDtypeStruct((m, d), F32), jax.ShapeDtypeStruct((m, d), BF16)],
        scratch_shapes=[pltpu.VMEM((CONV_CH // LANES, CONV_HALO + ts, LANES), F32),
                        pltpu.VMEM((ts, CONV_CH), F32),
                        pltpu.VMEM((ts, CONV_CH), BF16)],
        compiler_params=_params(("parallel",)),
        name="conformer_out",
    )(p, p, p, p, w_dw, b_dw.reshape(1, CONV_CH), ln_g.reshape(1, CONV_CH),
      ln_b.reshape(1, CONV_CH), ret_out, w_out, w_out, x, g.reshape(1, d))


def _retention_tables():
    t = RET_BLOCK
    log_gamma = jnp.log1p(-jnp.exp2(-5.0 - jnp.arange(RET_HEADS, dtype=F32)))
    pos = jnp.arange(t, dtype=F32)
    dist = jnp.abs(pos[:, None] - pos[None, :])
    chunk = jnp.arange(t) // CHUNK
    mask = chunk[None, :] <= chunk[:, None]
    d = jnp.where(mask[None], jnp.exp(dist[None] * log_gamma[:, None, None]), 0.0)
    xi = jnp.exp((pos + 1.0)[None] * log_gamma[:, None])
    zeta = jnp.exp((t - 1.0 - pos)[None] * log_gamma[:, None])
    gt = jnp.exp(t * log_gamma)
    xi = jnp.broadcast_to(xi[:, :, None], (RET_HEADS, t, RET_DV))
    zeta = jnp.broadcast_to(zeta[:, :, None], (RET_HEADS, t, RET_DK))
    gt = jnp.broadcast_to(gt[:, None, None], (RET_HEADS, 1, RET_DV))
    inv = ROPE_THETA ** (-jnp.arange(0, RET_DK, 2, dtype=F32) / RET_DK)
    ang = jnp.arange(SEQ, dtype=F32)[:, None] * inv[None, :]
    return d, xi, zeta, gt, jnp.cos(ang), jnp.sin(ang)


def _rotary(x, cos, sin):
    half = x.shape[-1] // 2
    x1, x2 = x[:, :half], x[:, half:]
    return jnp.concatenate([x1 * cos - x2 * sin, x1 * sin + x2 * cos], axis=-1)


def _retention_kernel(q_ref, k_ref, v_ref, gate_ref, cos_ref, sin_ref, d_ref, xi_ref, zeta_ref,
                      gt_ref, lg_ref, o_ref, state_ref):
    t = RET_BLOCK
    state_ref[...] = jnp.zeros_like(state_ref)
    scale = np.float32(RET_DK ** -0.5)

    def body(blk, carry):
        r0 = pl.multiple_of(blk * t, t)
        rows = pl.ds(r0, t)
        cos = cos_ref[rows, :]
        sin = sin_ref[rows, :]
        for hd in range(RET_HEADS_PER_STEP):
            kc = slice(hd * RET_DK, (hd + 1) * RET_DK)
            vc = slice(hd * RET_DV, (hd + 1) * RET_DV)
            q = _rotary(q_ref[rows, kc], cos, sin)
            k = _rotary(k_ref[rows, kc], cos, sin) * scale
            v = v_ref[rows, vc].astype(BF16)
            qb = q.astype(BF16)
            s = lax.dot_general(qb, k.astype(BF16), (((1,), (1,)), ((), ())),
                                preferred_element_type=F32) * d_ref[hd]
            state = state_ref[hd]
            o = jnp.dot(s.astype(BF16), v, preferred_element_type=F32)
            o = o + xi_ref[hd] * jnp.dot(qb, state.astype(BF16), preferred_element_type=F32)
            kz = (k * zeta_ref[hd]).astype(BF16)
            state_ref[hd] = gt_ref[hd] * state + lax.dot_general(
                kz, v, (((0,), (0,)), ((), ())), preferred_element_type=F32)
            mu = jnp.mean(o, axis=-1, keepdims=True)
            oc = o - mu
            var = jnp.mean(oc * oc, axis=-1, keepdims=True)
            on = oc * lax.rsqrt(var + LN_EPS) * lg_ref[:, vc]
            g = gate_ref[rows, vc]
            o_ref[rows, vc] = (on * (g * jax.nn.sigmoid(g))).astype(o_ref.dtype)
        return carry

    lax.fori_loop(0, SEQ // t, body, 0)


def retention(p, ln_g):
    m = p.shape[0]
    d, xi, zeta, gt, cos, sin = _retention_tables()
    t = RET_BLOCK
    hp = RET_HEADS_PER_STEP
    groups = RET_HEADS // hp
    col0 = 2 * CONV_CH // (hp * RET_DK)

    def col(off):
        return lambda b, h: (b, col0 + off * groups + h)

    return pl.pallas_call(
        _retention_kernel,
        grid=(m // SEQ, groups),
        in_specs=[pl.BlockSpec((SEQ, hp * RET_DK), col(0)),
                  pl.BlockSpec((SEQ, hp * RET_DK), col(1)),
                  pl.BlockSpec((SEQ, hp * RET_DV), col(2)),
                  pl.BlockSpec((SEQ, hp * RET_DV), col(3)),
                  pl.BlockSpec((SEQ, RET_DK // 2), lambda b, h: (0, 0)),
                  pl.BlockSpec((SEQ, RET_DK // 2), lambda b, h: (0, 0)),
                  pl.BlockSpec((hp, t, t), lambda b, h: (h, 0, 0)),
                  pl.BlockSpec((hp, t, RET_DV), lambda b, h: (h, 0, 0)),
                  pl.BlockSpec((hp, t, RET_DK), lambda b, h: (h, 0, 0)),
                  pl.BlockSpec((hp, 1, RET_DV), lambda b, h: (h, 0, 0)),
                  pl.BlockSpec((1, hp * RET_DV), lambda b, h: (0, h))],
        out_specs=pl.BlockSpec((SEQ, hp * RET_DV), lambda b, h: (b, h)),
        out_shape=jax.ShapeDtypeStruct((m, RET_HEADS * RET_DV), BF16),
        scratch_shapes=[pltpu.VMEM((hp, RET_DK, RET_DV), F32)],
        compiler_params=_params(("parallel", "arbitrary")),
        name="retention",
    )(p, p, p, p, cos, sin, d, xi, zeta, gt, ln_g.reshape(1, RET_HEADS * RET_DV))


def _sgu_out_kernel(u_ref, v_ref, lg_ref, lb_ref, ws_ref, bs_ref, w_ref, x_ref, g_ref,
                    xo_ref, ho_ref, vn_ref, z_ref, *, tm, rc):
    ri = lax.broadcasted_iota(jnp.int32, (GM_BLOCK, GM_BLOCK), 0) // CHUNK
    ci = lax.broadcasted_iota(jnp.int32, (GM_BLOCK, GM_BLOCK), 1) // CHUNK
    gw = GM_WIDTH // GM_GROUPS
    wm = [jnp.where(ci <= ri, ws_ref[g], 0.0).astype(BF16) for g in range(GM_GROUPS)]
    for r in range(tm // rc):
        for n in range(r * rc // GM_BLOCK, (r + 1) * rc // GM_BLOCK):
            blk = slice(n * GM_BLOCK, (n + 1) * GM_BLOCK)
            v = v_ref[blk, :]
            mu = jnp.mean(v, axis=-1, keepdims=True)
            vc = v - mu
            var = jnp.mean(vc * vc, axis=-1, keepdims=True)
            vn_ref[blk, :] = (vc * lax.rsqrt(var + LN_EPS) * lg_ref[...] + lb_ref[...]).astype(BF16)
            for g in range(GM_GROUPS):
                cols = slice(g * gw, (g + 1) * gw)
                mixed = jnp.dot(wm[g], vn_ref[blk, cols], preferred_element_type=F32) + bs_ref[g]
                z_ref[blk, cols] = (u_ref[blk, cols] * mixed).astype(BF16)
        rows = slice(r * rc, (r + 1) * rc)
        xn = x_ref[rows, :] + jnp.dot(z_ref[rows, :], w_ref[...], preferred_element_type=F32)
        _rms_finish(xn, g_ref, xo_ref, ho_ref, rows)


def spatial_gate_out(p, ln_g, ln_b, ws, bs, w_out, layer, x, g, *, tm=512, rc=256):
    m = p.shape[0]
    d = w_out.shape[2]
    row = lambda i: (i, 0)
    fixed2 = lambda i: (0, 0)
    return pl.pallas_call(
        functools.partial(_sgu_out_kernel, tm=tm, rc=rc),
        grid=(m // tm,),
        in_specs=[pl.BlockSpec((tm, GM_WIDTH), row),
                  pl.BlockSpec((tm, GM_WIDTH), lambda i: (i, 1)),
                  pl.BlockSpec((1, GM_WIDTH), fixed2),
                  pl.BlockSpec((1, GM_WIDTH), fixed2),
                  pl.BlockSpec((GM_GROUPS, GM_BLOCK, GM_BLOCK), lambda i: (0, 0, 0)),
                  pl.BlockSpec((GM_GROUPS, GM_BLOCK, 1), lambda i: (0, 0, 0)),
                  pl.BlockSpec((None, GM_WIDTH, d), lambda i: (layer, 0, 0),
                               pipeline_mode=pl.Buffered(1)),
                  pl.BlockSpec((tm, d), row),
                  pl.BlockSpec((1, d), fixed2)],
        out_specs=[pl.BlockSpec((tm, d), row), pl.BlockSpec((tm, d), row)],
        out_shape=[jax.ShapeDtypeStruct((m, d), F32), jax.ShapeDtypeStruct((m, d), BF16)],
        scratch_shapes=[pltpu.VMEM((tm, GM_WIDTH), BF16), pltpu.VMEM((tm, GM_WIDTH), BF16)],
        compiler_params=_params(("parallel",)),
        name="spatial_gate_out",
    )(p, p, ln_g.reshape(1, GM_WIDTH), ln_b.reshape(1, GM_WIDTH), ws,
      bs.reshape(GM_GROUPS, GM_BLOCK, 1), w_out, x, g.reshape(1, d))


def kernel(x, mix_norm_g, ffn_norm_g, final_norm_g, ev_w_in, ev_conv_dw_w, ev_conv_dw_b,
           ev_conv_ln_g, ev_conv_ln_b, ev_ret_ln_g, ev_w_out, od_w_in, od_gm_ln_g, od_gm_ln_b,
           od_gm_ws, od_gm_bs, od_w_out, ffn_w_up, ffn_dw_w, ffn_dw_b, ffn_w_down):
    bsz, s, d = x.shape
    xf = x.reshape(bsz * s, d)
    ev_w_in, ev_w_out, od_w_in, od_w_out, ffn_w_up, ffn_w_down = (
        w.astype(BF16) for w in (ev_w_in, ev_w_out, od_w_in, od_w_out, ffn_w_up, ffn_w_down))
    h = None
    for i in range(DEPTH):
        j = i // 2
        if i % 2 == 0:
            if h is None:
                p = norm_matmul(xf, mix_norm_g[i], ev_w_in, j, name="even_in")
            else:
                p = matmul(h, ev_w_in, j, tm=SEQ, name="even_in")
            b_out = retention(p, ev_ret_ln_g[j])
            xf, h = conformer_out(p, ev_conv_dw_w[j], ev_conv_dw_b[j], ev_conv_ln_g[j],
                                  ev_conv_ln_b[j], b_out, ev_w_out, j, xf, ffn_norm_g[i])
        else:
            p = matmul(h, od_w_in, j, act="gelu", name="odd_in")
            xf, h = spatial_gate_out(p, od_gm_ln_g[j], od_gm_ln_b[j], od_gm_ws[j], od_gm_bs[j],
                                     od_w_out, j, xf, ffn_norm_g[i])
        gated = ffn_up_gate(h, ffn_w_up, i, ffn_dw_w[i], ffn_dw_b[i])
        if i < DEPTH - 1:
            xf, h = matmul_residual_norm(gated, ffn_w_down, i, xf, mix_norm_g[i + 1],
                                         name="ffn_down")
        else:
            h = matmul_residual_norm(gated, ffn_w_down, i, xf, final_norm_g, h_dtype=F32,
                                     emit_x=False, name="ffn_down")
    return h.reshape(bsz, s, d)
```

```python
import functools

import jax
import jax.numpy as jnp
import numpy as np
from jax import lax
from jax.experimental import pallas as pl
from jax.experimental.pallas import tpu as pltpu

D_MODEL = 2048
SEQ = 2048
DEPTH = 4
CHUNK = 64
CONV_CH = 1024
CONV_K = 31
RET_HEADS = 4
RET_DK = 256
RET_DV = 256
GM_WIDTH = 2048
GM_BLOCK = 128
GM_GROUPS = 4
D_FF = 5632
FFN_K = 3
ROPE_THETA = 10000.0
RMS_EPS = 1e-6
LN_EPS = 1e-5

RET_BLOCK = 256
RET_HEADS_PER_STEP = 2
CONV_HALO = 32
LANES = 128

VMEM_LIMIT = 56 * 1024 * 1024

BF16 = jnp.bfloat16
F32 = jnp.float32


def _params(sem):
    return pltpu.CompilerParams(dimension_semantics=sem, vmem_limit_bytes=VMEM_LIMIT)


def _norm_mm_kernel(x_ref, g_ref, w_ref, o_ref, h_ref):
    @pl.when(pl.program_id(1) == 0)
    def _():
        x = x_ref[...]
        ms = jnp.mean(x * x, axis=-1, keepdims=True)
        h_ref[...] = (x * lax.rsqrt(ms + RMS_EPS) * g_ref[...]).astype(h_ref.dtype)

    o_ref[...] = jnp.dot(h_ref[...], w_ref[...], preferred_element_type=F32).astype(o_ref.dtype)


def norm_matmul(x, g, w, layer, *, out_dtype=F32, tm=1024, tn=1024, name="norm_proj"):
    m, k = x.shape
    n = w.shape[2]
    return pl.pallas_call(
        _norm_mm_kernel,
        grid=(m // tm, n // tn),
        in_specs=[pl.BlockSpec((tm, k), lambda i, j: (i, 0)),
                  pl.BlockSpec((1, k), lambda i, j: (0, 0)),
                  pl.BlockSpec((None, k, tn), lambda i, j: (layer, 0, j))],
        out_specs=pl.BlockSpec((tm, tn), lambda i, j: (i, j)),
        out_shape=jax.ShapeDtypeStruct((m, n), out_dtype),
        scratch_shapes=[pltpu.VMEM((tm, k), BF16)],
        compiler_params=_params(("parallel", "arbitrary")),
        name=name,
    )(x, g.reshape(1, k), w)


def _mm_kernel(a_ref, w_ref, o_ref, *, act):
    acc = jnp.dot(a_ref[...], w_ref[...], preferred_element_type=F32)
    if act == "gelu":
        acc = 0.5 * acc * (1.0 + lax.erf(acc * np.float32(np.sqrt(0.5))))
    o_ref[...] = acc.astype(o_ref.dtype)


def matmul(a, w, layer, *, act=None, out_dtype=F32, tm=1024, tn=1024, name="proj"):
    m, k = a.shape
    n = w.shape[2]
    return pl.pallas_call(
        functools.partial(_mm_kernel, act=act),
        grid=(m // tm, n // tn),
        in_specs=[pl.BlockSpec((tm, k), lambda i, j: (i, 0)),
                  pl.BlockSpec((None, k, tn), lambda i, j: (layer, 0, j))],
        out_specs=pl.BlockSpec((tm, tn), lambda i, j: (i, j)),
        out_shape=jax.ShapeDtypeStruct((m, n), out_dtype),
        compiler_params=_params(("parallel", "arbitrary")),
        name=name,
    )(a, w)


def _rms_finish(xn, g_ref, xo_ref, ho_ref, rows):
    if xo_ref is not None:
        xo_ref[rows, :] = xn
    ms = jnp.mean(xn * xn, axis=-1, keepdims=True)
    ho_ref[rows, :] = (xn * lax.rsqrt(ms + RMS_EPS) * g_ref[...]).astype(ho_ref.dtype)


def _mm_res_kernel(a_ref, w_ref, x_ref, g_ref, *out_refs, tm, rc):
    ho_ref = out_refs[-1]
    xo_ref = out_refs[0] if len(out_refs) == 2 else None
    for r in range(tm // rc):
        rows = slice(r * rc, (r + 1) * rc)
        xn = x_ref[rows, :] + jnp.dot(a_ref[rows, :], w_ref[...], preferred_element_type=F32)
        _rms_finish(xn, g_ref, xo_ref, ho_ref, rows)


def matmul_residual_norm(a, w, layer, x, g, *, h_dtype=BF16, emit_x=True, tm=512, rc=256,
                         name="proj_res"):
    m, ka = a.shape
    d = w.shape[2]
    row = lambda i: (i, 0)
    fixed = lambda i: (0, 0)
    h_shape = jax.ShapeDtypeStruct((m, d), h_dtype)
    if emit_x:
        out_specs = [pl.BlockSpec((tm, d), row), pl.BlockSpec((tm, d), row)]
        out_shape = [jax.ShapeDtypeStruct((m, d), F32), h_shape]
    else:
        out_specs = pl.BlockSpec((tm, d), row)
        out_shape = h_shape
    return pl.pallas_call(
        functools.partial(_mm_res_kernel, tm=tm, rc=rc),
        grid=(m // tm,),
        in_specs=[pl.BlockSpec((tm, ka), row),
                  pl.BlockSpec((None, ka, d), lambda i: (layer, 0, 0),
                               pipeline_mode=pl.Buffered(1)),
                  pl.BlockSpec((tm, d), row),
                  pl.BlockSpec((1, d), fixed)],
        out_specs=out_specs,
        out_shape=out_shape,
        compiler_params=_params(("parallel",)),
        name=name,
    )(a, w, x, g.reshape(1, d))


FFN_PAD = 8
FFN_ROW_CHUNKS = (1024, 1024)


def _ffn_up_kernel(h_ref, wa_ref, wb_ref, cwa_ref, cwb_ref, cba_ref, cbb_ref, o_ref, s_ref,
                   *, chunks, tn):
    nt = tn // LANES
    w_refs = (wa_ref, wb_ref)
    cw = (cwa_ref[...], cwb_ref[...])
    cb = (cba_ref[...], cbb_ref[...])
    for ab in range(2):
        for c in range(nt):
            s_ref[ab, c, 0:FFN_PAD, :] = jnp.zeros((FFN_PAD, LANES), F32)
    r0 = 0
    for rc in chunks:
        hr = h_ref[r0:r0 + rc, :]
        u = [jnp.dot(hr, w_refs[ab][...], preferred_element_type=F32) for ab in range(2)]
        for ab in range(2):
            for c in range(nt):
                s_ref[ab, c, FFN_PAD + r0:FFN_PAD + r0 + rc, :] = u[ab][:, c * LANES:(c + 1) * LANES]
        for c in range(nt):
            lanes = slice(c * LANES, (c + 1) * LANES)
            y = []
            for ab in range(2):
                u1 = s_ref[ab, c, FFN_PAD + r0 - 1:FFN_PAD + r0 - 1 + rc, :]
                u2 = s_ref[ab, c, FFN_PAD + r0 - 2:FFN_PAD + r0 - 2 + rc, :]
                y.append(cw[ab][2:3, lanes] * u[ab][:, lanes] + cw[ab][1:2, lanes] * u1
                         + cw[ab][0:1, lanes] * u2 + cb[ab][:, lanes])
            a, b = y
            o_ref[r0:r0 + rc, lanes] = (a * jax.nn.sigmoid(a) * b).astype(o_ref.dtype)
        r0 += rc


def ffn_up_gate(h, w_up, layer, dw_w, dw_b, *, tn=512):
    m, k = h.shape
    nj = D_FF // tn
    dw_b = dw_b.reshape(1, 2 * D_FF)
    assert sum(FFN_ROW_CHUNKS) == SEQ
    return pl.pallas_call(
        functools.partial(_ffn_up_kernel, chunks=FFN_ROW_CHUNKS, tn=tn),
        grid=(m // SEQ, nj),
        in_specs=[pl.BlockSpec((SEQ, k), lambda i, j: (i, 0)),
                  pl.BlockSpec((None, k, tn), lambda i, j: (layer, 0, j)),
                  pl.BlockSpec((None, k, tn), lambda i, j: (layer, 0, j + nj)),
                  pl.BlockSpec((FFN_K, tn), lambda i, j: (0, j)),
                  pl.BlockSpec((FFN_K, tn), lambda i, j: (0, j + nj)),
                  pl.BlockSpec((1, tn), lambda i, j: (0, j)),
                  pl.BlockSpec((1, tn), lambda i, j: (0, j + nj))],
        out_specs=pl.BlockSpec((SEQ, tn), lambda i, j: (i, j)),
        out_shape=jax.ShapeDtypeStruct((m, D_FF), BF16),
        scratch_shapes=[pltpu.VMEM((2, tn // LANES, FFN_PAD + SEQ, LANES), F32)],
        compiler_params=_params(("parallel", "arbitrary")),
        name="ffn_up_gate",
    )(h, w_up, w_up, dw_w, dw_w, dw_b, dw_b)


def _conformer_out_kernel(ga_ref, gb_ref, ha_ref, hb_ref, w_ref, b_ref, lg_ref, lb_ref, r_ref,
                          wa_ref, wb_ref, x_ref, g_ref, xo_ref, ho_ref, u_ref, y_ref, a_ref,
                          *, ts, rc, sub, tiles_per_seq):
    i = pl.program_id(0)
    nc = CONV_CH // LANES
    not_first = (i % tiles_per_seq != 0).astype(F32)
    uh = ha_ref[...] * jax.nn.sigmoid(hb_ref[...]) * not_first
    u = ga_ref[...] * jax.nn.sigmoid(gb_ref[...])
    for c in range(nc):
        lanes = slice(c * LANES, (c + 1) * LANES)
        u_ref[c, 0:CONV_HALO, :] = uh[:, lanes]
        u_ref[c, CONV_HALO:CONV_HALO + ts, :] = u[:, lanes]
    base = CONV_HALO - (CONV_K - 1)
    for r in range(ts // rc):
        for c in range(nc):
            lanes = slice(c * LANES, (c + 1) * LANES)
            for q in range(rc // sub):
                r0 = r * rc + q * sub
                acc = jnp.broadcast_to(b_ref[:, lanes], (sub, LANES))
                for k in range(CONV_K):
                    acc = acc + w_ref[k:k + 1, lanes] * u_ref[c, r0 + base + k:r0 + base + k + sub, :]
                y_ref[r0:r0 + sub, lanes] = acc
        rows = slice(r * rc, (r + 1) * rc)
        y = y_ref[rows, :]
        mu = jnp.mean(y, axis=-1, keepdims=True)
        yc = y - mu
        var = jnp.mean(yc * yc, axis=-1, keepdims=True)
        yn = yc * lax.rsqrt(var + LN_EPS) * lg_ref[...] + lb_ref[...]
        a_ref[rows, :] = (yn * jax.nn.sigmoid(yn)).astype(BF16)
        xn = (x_ref[rows, :]
              + jnp.dot(a_ref[rows, :], wa_ref[...], preferred_element_type=F32)
              + jnp.dot(r_ref[rows, :], wb_ref[...], preferred_element_type=F32))
        _rms_finish(xn, g_ref, xo_ref, ho_ref, rows)


def conformer_out(p, w_dw, b_dw, ln_g, ln_b, ret_out, w_out, layer, x, g, *, ts=512, rc=256,
                  sub=128):
    m = p.shape[0]
    d = w_out.shape[2]
    tiles_per_seq = SEQ // ts
    halo_per_tile = ts // CONV_HALO
    row = lambda i: (i, 0)
    fixed2 = lambda i: (0, 0)

    def halo_idx(col):
        return lambda i: (jnp.maximum(i * halo_per_tile - 1, 0), col)

    return pl.pallas_call(
        functools.partial(_conformer_out_kernel, ts=ts, rc=rc, sub=sub,
                          tiles_per_seq=tiles_per_seq),
        grid=(m // ts,),
        in_specs=[pl.BlockSpec((ts, CONV_CH), row),
                  pl.BlockSpec((ts, CONV_CH), lambda i: (i, 1)),
                  pl.BlockSpec((CONV_HALO, CONV_CH), halo_idx(0)),
                  pl.BlockSpec((CONV_HALO, CONV_CH), halo_idx(1)),
                  pl.BlockSpec((CONV_K, CONV_CH), fixed2),
                  pl.BlockSpec((1, CONV_CH), fixed2),
                  pl.BlockSpec((1, CONV_CH), fixed2),
                  pl.BlockSpec((1, CONV_CH), fixed2),
                  pl.BlockSpec((ts, CONV_CH), row),
                  pl.BlockSpec((None, CONV_CH, d), lambda i: (layer, 0, 0),
                               pipeline_mode=pl.Buffered(1)),
                  pl.BlockSpec((None, CONV_CH, d), lambda i: (layer, 1, 0),
                               pipeline_mode=pl.Buffered(1)),
                  pl.BlockSpec((ts, d), row),
                  pl.BlockSpec((1, d), fixed2)],
        out_specs=[pl.BlockSpec((ts, d), row), pl.BlockSpec((ts, d), row)],
        out_shape=[jax.ShapeDtypeStruct((m, d), F32), jax.ShapeDtypeStruct((m, d), BF16)],
        scratch_shapes=[pltpu.VMEM((CONV_CH // LANES, CONV_HALO + ts, LANES), F32),
                        pltpu.VMEM((ts, CONV_CH), F32),
                        pltpu.VMEM((ts, CONV_CH), BF16)],
        compiler_params=_params(("parallel",)),
        name="conformer_out",
    )(p, p, p, p, w_dw, b_dw.reshape(1, CONV_CH), ln_g.reshape(1, CONV_CH),
      ln_b.reshape(1, CONV_CH), ret_out, w_out, w_out, x, g.reshape(1, d))


def _retention_tables():
    t = RET_BLOCK
    log_gamma = jnp.log1p(-jnp.exp2(-5.0 - jnp.arange(RET_HEADS, dtype=F32)))
    pos = jnp.arange(t, dtype=F32)
    dist = jnp.abs(pos[:, None] - pos[None, :])
    chunk = jnp.arange(t) // CHUNK
    mask = chunk[None, :] <= chunk[:, None]
    d = jnp.where(mask[None], jnp.exp(dist[None] * log_gamma[:, None, None]), 0.0)
    xi = jnp.exp((pos + 1.0)[None] * log_gamma[:, None])
    zeta = jnp.exp((t - 1.0 - pos)[None] * log_gamma[:, None])
    gt = jnp.exp(t * log_gamma)
    xi = jnp.broadcast_to(xi[:, :, None], (RET_HEADS, t, RET_DV))
    zeta = jnp.broadcast_to(zeta[:, :, None], (RET_HEADS, t, RET_DK))
    gt = jnp.broadcast_to(gt[:, None, None], (RET_HEADS, 1, RET_DV))
    inv = ROPE_THETA ** (-jnp.arange(0, RET_DK, 2, dtype=F32) / RET_DK)
    ang = jnp.arange(SEQ, dtype=F32)[:, None] * inv[None, :]
    return d, xi, zeta, gt, jnp.cos(ang), jnp.sin(ang)


def _rotary(x, cos, sin):
    half = x.shape[-1] // 2
    x1, x2 = x[:, :half], x[:, half:]
    return jnp.concatenate([x1 * cos - x2 * sin, x1 * sin + x2 * cos], axis=-1)


def _retention_kernel(q_ref, k_ref, v_ref, gate_ref, cos_ref, sin_ref, d_ref, xi_ref, zeta_ref,
                      gt_ref, lg_ref, o_ref, state_ref):
    t = RET_BLOCK
    state_ref[...] = jnp.zeros_like(state_ref)
    scale = np.float32(RET_DK ** -0.5)

    def body(blk, carry):
        r0 = pl.multiple_of(blk * t, t)
        rows = pl.ds(r0, t)
        cos = cos_ref[rows, :]
        sin = sin_ref[rows, :]
        for hd in range(RET_HEADS_PER_STEP):
            kc = slice(hd * RET_DK, (hd + 1) * RET_DK)
            vc = slice(hd * RET_DV, (hd + 1) * RET_DV)
            q = _rotary(q_ref[rows, kc], cos, sin)
            k = _rotary(k_ref[rows, kc], cos, sin) * scale
            v = v_ref[rows, vc].astype(BF16)
            qb = q.astype(BF16)
            s = lax.dot_general(qb, k.astype(BF16), (((1,), (1,)), ((), ())),
                                preferred_element_type=F32) * d_ref[hd]
            state = state_ref[hd]
            o = jnp.dot(s.astype(BF16), v, preferred_element_type=F32)
            o = o + xi_ref[hd] * jnp.dot(qb, state.astype(BF16), preferred_element_type=F32)
            kz = (k * zeta_ref[hd]).astype(BF16)
            state_ref[hd] = gt_ref[hd] * state + lax.dot_general(
                kz, v, (((0,), (0,)), ((), ())), preferred_element_type=F32)
            mu = jnp.mean(o, axis=-1, keepdims=True)
            oc = o - mu
            var = jnp.mean(oc * oc, axis=-1, keepdims=True)
            on = oc * lax.rsqrt(var + LN_EPS) * lg_ref[:, vc]
            g = gate_ref[rows, vc]
            o_ref[rows, vc] = (on * (g * jax.nn.sigmoid(g))).astype(o_ref.dtype)
        return carry

    lax.fori_loop(0, SEQ // t, body, 0)


def retention(p, ln_g):
    m = p.shape[0]
    d, xi, zeta, gt, cos, sin = _retention_tables()
    t = RET_BLOCK
    hp = RET_HEADS_PER_STEP
    groups = RET_HEADS // hp
    col0 = 2 * CONV_CH // (hp * RET_DK)

    def col(off):
        return lambda b, h: (b, col0 + off * groups + h)

    return pl.pallas_call(
        _retention_kernel,
        grid=(m // SEQ, groups),
        in_specs=[pl.BlockSpec((SEQ, hp * RET_DK), col(0)),
                  pl.BlockSpec((SEQ, hp * RET_DK), col(1)),
                  pl.BlockSpec((SEQ, hp * RET_DV), col(2)),
                  pl.BlockSpec((SEQ, hp * RET_DV), col(3)),
                  pl.BlockSpec((SEQ, RET_DK // 2), lambda b, h: (0, 0)),
                  pl.BlockSpec((SEQ, RET_DK // 2), lambda b, h: (0, 0)),
                  pl.BlockSpec((hp, t, t), lambda b, h: (h, 0, 0)),
                  pl.BlockSpec((hp, t, RET_DV), lambda b, h: (h, 0, 0)),
                  pl.BlockSpec((hp, t, RET_DK), lambda b, h: (h, 0, 0)),
                  pl.BlockSpec((hp, 1, RET_DV), lambda b, h: (h, 0, 0)),
                  pl.BlockSpec((1, hp * RET_DV), lambda b, h: (0, h))],
        out_specs=pl.BlockSpec((SEQ, hp * RET_DV), lambda b, h: (b, h)),
        out_shape=jax.ShapeDtypeStruct((m, RET_HEADS * RET_DV), BF16),
        scratch_shapes=[pltpu.VMEM((hp, RET_DK, RET_DV), F32)],
        compiler_params=_params(("parallel", "arbitrary")),
        name="retention",
    )(p, p, p, p, cos, sin, d, xi, zeta, gt, ln_g.reshape(1, RET_HEADS * RET_DV))


def _sgu_out_kernel(u_ref, v_ref, lg_ref, lb_ref, ws_ref, bs_ref, w_ref, x_ref, g_ref,
                    xo_ref, ho_ref, vn_ref, z_ref, *, tm, rc):
    ri = lax.broadcasted_iota(jnp.int32, (GM_BLOCK, GM_BLOCK), 0) // CHUNK
    ci = lax.broadcasted_iota(jnp.int32, (GM_BLOCK, GM_BLOCK), 1) // CHUNK
    gw = GM_WIDTH // GM_GROUPS
    wm = [jnp.where(ci <= ri, ws_ref[g], 0.0).astype(BF16) for g in range(GM_GROUPS)]
    for r in range(tm // rc):
        for n in range(r * rc // GM_BLOCK, (r + 1) * rc // GM_BLOCK):
            blk = slice(n * GM_BLOCK, (n + 1) * GM_BLOCK)
            v = v_ref[blk, :]
            mu = jnp.mean(v, axis=-1, keepdims=True)
            vc = v - mu
            var = jnp.mean(vc * vc, axis=-1, keepdims=True)
            vn_ref[blk, :] = (vc * lax.rsqrt(var + LN_EPS) * lg_ref[...] + lb_ref[...]).astype(BF16)
            for g in range(GM_GROUPS):
                cols = slice(g * gw, (g + 1) * gw)
                mixed = jnp.dot(wm[g], vn_ref[blk, cols], preferred_element_type=F32) + bs_ref[g]
                z_ref[blk, cols] = (u_ref[blk, cols] * mixed).astype(BF16)
        rows = slice(r * rc, (r + 1) * rc)
        xn = x_ref[rows, :] + jnp.dot(z_ref[rows, :], w_ref[...], preferred_element_type=F32)
        _rms_finish(xn, g_ref, xo_ref, ho_ref, rows)


def spatial_gate_out(p, ln_g, ln_b, ws, bs, w_out, layer, x, g, *, tm=512, rc=256):
    m = p.shape[0]
    d = w_out.shape[2]
    row = lambda i: (i, 0)
    fixed2 = lambda i: (0, 0)
    return pl.pallas_call(
        functools.partial(_sgu_out_kernel, tm=tm, rc=rc),
        grid=(m // tm,),
        in_specs=[pl.BlockSpec((tm, GM_WIDTH), row),
                  pl.BlockSpec((tm, GM_WIDTH), lambda i: (i, 1)),
                  pl.BlockSpec((1, GM_WIDTH), fixed2),
                  pl.BlockSpec((1, GM_WIDTH), fixed2),
                  pl.BlockSpec((GM_GROUPS, GM_BLOCK, GM_BLOCK), lambda i: (0, 0, 0)),
                  pl.BlockSpec((GM_GROUPS, GM_BLOCK, 1), lambda i: (0, 0, 0)),
                  pl.BlockSpec((None, GM_WIDTH, d), lambda i: (layer, 0, 0),
                               pipeline_mode=pl.Buffered(1)),
                  pl.BlockSpec((tm, d), row),
                  pl.BlockSpec((1, d), fixed2)],
        out_specs=[pl.BlockSpec((tm, d), row), pl.BlockSpec((tm, d), row)],
        out_shape=[jax.ShapeDtypeStruct((m, d), F32), jax.ShapeDtypeStruct((m, d), BF16)],
        scratch_shapes=[pltpu.VMEM((tm, GM_WIDTH), BF16), pltpu.VMEM((tm, GM_WIDTH), BF16)],
        compiler_params=_params(("parallel",)),
        name="spatial_gate_out",
    )(p, p, ln_g.reshape(1, GM_WIDTH), ln_b.reshape(1, GM_WIDTH), ws,
      bs.reshape(GM_GROUPS, GM_BLOCK, 1), w_out, x, g.reshape(1, d))


def kernel(x, mix_norm_g, ffn_norm_g, final_norm_g, ev_w_in, ev_conv_dw_w, ev_conv_dw_b,
           ev_conv_ln_g, ev_conv_ln_b, ev_ret_ln_g, ev_w_out, od_w_in, od_gm_ln_g, od_gm_ln_b,
           od_gm_ws, od_gm_bs, od_w_out, ffn_w_up, ffn_dw_w, ffn_dw_b, ffn_w_down):
    bsz, s, d = x.shape
    xf = x.reshape(bsz * s, d)
    ev_w_in, ev_w_out, od_w_in, od_w_out, ffn_w_up, ffn_w_down = (
        w.astype(BF16) for w in (ev_w_in, ev_w_out, od_w_in, od_w_out, ffn_w_up, ffn_w_down))
    h = None
    for i in range(DEPTH):
        j = i // 2
        if i % 2 == 0:
            if h is None:
                p = norm_matmul(xf, mix_norm_g[i], ev_w_in, j, tn=2048, name="even_in")
            else:
                p = matmul(h, ev_w_in, j, tm=SEQ, name="even_in")
            b_out = retention(p, ev_ret_ln_g[j])
            xf, h = conformer_out(p, ev_conv_dw_w[j], ev_conv_dw_b[j], ev_conv_ln_g[j],
                                  ev_conv_ln_b[j], b_out, ev_w_out, j, xf, ffn_norm_g[i])
        else:
            p = matmul(h, od_w_in, j, act="gelu", name="odd_in")
            xf, h = spatial_gate_out(p, od_gm_ln_g[j], od_gm_ln_b[j], od_gm_ws[j], od_gm_bs[j],
                                     od_w_out, j, xf, ffn_norm_g[i])
        gated = ffn_up_gate(h, ffn_w_up, i, ffn_dw_w[i], ffn_dw_b[i])
        if i < DEPTH - 1:
            xf, h = matmul_residual_norm(gated, ffn_w_down, i, xf, mix_norm_g[i + 1],
                                         name="ffn_down")
        else:
            h = matmul_residual_norm(gated, ffn_w_down, i, xf, final_norm_g, h_dtype=F32,
                                     emit_x=False, name="ffn_down")
    return h.reshape(bsz, s, d)
```
